```python
import jax, jax.numpy as jnp
from jax import lax
import numpy as np

D_MODEL = 2048
BATCH = 2
SEQ = 4096
DEPTH = 2
DEC_BATCH = 1
DEC_SEQ = 8192
PAST_LEN = 128

N_META = 16
D_SCONV = 512
N_Q_HEADS = 16
N_KV_HEADS = 4
HEAD_DIM = 64
GQA_GROUP = N_Q_HEADS // N_KV_HEADS
D_ATTN = N_Q_HEADS * HEAD_DIM
D_KV = N_KV_HEADS * HEAD_DIM
D_CONF = 512
D_MIX = D_SCONV + D_ATTN + D_CONF
SHORT_CONV_W = 3
CONF_CONV_W = 31
WINDOW = 128
BLOCK = 128
N_KEYS = 128
N_EXPERTS = N_KEYS * N_KEYS
PEER_HEADS = 8
PEER_TOPK = 16
D_PKEY = 256
D_HALF = D_PKEY // 2
RMS_EPS = 1e-6
LN_EPS = 1e-5
NEG = -1e30
IN_SPLIT_SIZES = [D_SCONV, D_SCONV, D_SCONV, D_ATTN, D_KV, D_KV, D_CONF, D_CONF]
IN_SPLIT_IDX = [int(i) for i in np.cumsum(IN_SPLIT_SIZES)[:-1]]
D_IN_PROJ = int(sum(IN_SPLIT_SIZES))

kernel_name = "hymba_parallel_conv_swa_conformer_peer_encoder"


def rmsnorm(x, g):
    xf = x.astype(jnp.float32)
    y = xf * lax.rsqrt(jnp.mean(xf * xf, axis=-1, keepdims=True) + RMS_EPS)
    return (y * g.astype(jnp.float32)).astype(x.dtype)


def layernorm(x, g, b):
    xf = x.astype(jnp.float32)
    mu = jnp.mean(xf, axis=-1, keepdims=True)
    var = jnp.mean(jnp.square(xf - mu), axis=-1, keepdims=True)
    y = (xf - mu) * lax.rsqrt(var + LN_EPS)
    return (y * g.astype(jnp.float32) + b.astype(jnp.float32)).astype(x.dtype)


def depthwise_conv(x, w):
    K, C = w.shape
    return lax.conv_general_dilated(
        x, w[:, None, :].astype(x.dtype), window_strides=(1,),
        padding=[(K // 2, K // 2)], dimension_numbers=('NWC', 'WIO', 'NWC'),
        feature_group_count=C)


def alibi_slopes():
    return jnp.exp2(-8.0 * jnp.arange(1, N_Q_HEADS + 1, dtype=jnp.float32) / N_Q_HEADS)


def windowed_attention(q, k, v, sink):
    B, L = q.shape[0], q.shape[1]
    pad = BLOCK - N_META
    Lp = L + pad
    nb = Lp // BLOCK
    qb = jnp.pad(q, ((0, 0), (pad, 0), (0, 0), (0, 0))).reshape(B, nb, BLOCK, N_KV_HEADS, GQA_GROUP, HEAD_DIM)

    def neighbourhood(t):
        tp = jnp.pad(t, ((0, 0), (pad + BLOCK, BLOCK), (0, 0), (0, 0)))
        tb = tp.reshape(B, nb + 2, BLOCK, N_KV_HEADS, HEAD_DIM)
        return jnp.concatenate([tb[:, :-2], tb[:, 1:-1], tb[:, 2:]], axis=2)

    kb, vb = neighbourhood(k), neighbourhood(v)
    k_meta, v_meta = k[:, :N_META], v[:, :N_META]
    scale = HEAD_DIM ** -0.5

    qpos = jnp.arange(nb)[:, None] * BLOCK + jnp.arange(BLOCK)[None, :] - pad
    kpos = (jnp.arange(nb)[:, None] - 1) * BLOCK + jnp.arange(3 * BLOCK)[None, :] - pad
    dist = jnp.abs(qpos[:, :, None] - kpos[:, None, :])
    valid = (kpos[:, None, :] >= N_META) & (kpos[:, None, :] < L) & (dist <= WINDOW)
    mdist = jnp.minimum(jnp.abs(qpos[:, :, None] - jnp.arange(N_META)[None, None, :]), WINDOW)
    slopes = alibi_slopes().reshape(N_KV_HEADS, GQA_GROUP)[None, :, :, None, None]

    s_band = jnp.einsum('bnqhgd,bnkhd->bnhgqk', qb, kb, preferred_element_type=jnp.float32) * scale
    s_band = s_band - slopes * dist[:, None, None].astype(jnp.float32)
    s_band = jnp.where(valid[:, None, None], s_band, NEG)
    s_meta = jnp.einsum('bnqhgd,bmhd->bnhgqm', qb, k_meta, preferred_element_type=jnp.float32) * scale
    s_meta = s_meta - slopes * mdist[:, None, None].astype(jnp.float32)
    s_sink = jnp.broadcast_to(sink.astype(jnp.float32).reshape(N_KV_HEADS, GQA_GROUP)[None, None, :, :, None, None],
                              s_meta.shape[:-1] + (1,))
    p = jax.nn.softmax(jnp.concatenate([s_band, s_meta, s_sink], axis=-1), axis=-1)
    p_band = p[..., :3 * BLOCK].astype(v.dtype)
    p_meta = p[..., 3 * BLOCK:3 * BLOCK + N_META].astype(v.dtype)
    out = (jnp.einsum('bnhgqk,bnkhd->bnqhgd', p_band, vb)
           + jnp.einsum('bnhgqm,bmhd->bnqhgd', p_meta, v_meta))
    return out.reshape(B, Lp, D_ATTN)[:, pad:]


def mixer_layer(h, w_in, conv_a_w, attn_sink, conf_dw_w, conf_dw_b, conf_ln_g, conf_ln_b,
                g_out_a, g_out_b, g_out_c, w_out):
    B, L, _ = h.shape
    z = h @ w_in
    a_x, a_bg, a_cg, q, k, v, c_a, c_g = jnp.split(z, IN_SPLIT_IDX, axis=-1)
    y_a = a_bg * depthwise_conv(a_cg * a_x, conv_a_w)
    y_b = windowed_attention(q.reshape(B, L, N_Q_HEADS, HEAD_DIM),
                             k.reshape(B, L, N_KV_HEADS, HEAD_DIM),
                             v.reshape(B, L, N_KV_HEADS, HEAD_DIM), attn_sink)
    glu = c_a * jax.nn.sigmoid(c_g)
    y_c = jax.nn.silu(layernorm(depthwise_conv(glu, conf_dw_w) + conf_dw_b, conf_ln_g, conf_ln_b))
    merged = jnp.concatenate([rmsnorm(y_a, g_out_a), rmsnorm(y_b, g_out_b), rmsnorm(y_c, g_out_c)], axis=-1)
    return merged @ w_out


def peer_layer(h, wq, sub_keys, u, v):
    B, L, D = h.shape
    T = B * L
    hf = h.reshape(T, D)
    q = (hf @ wq).reshape(T, PEER_HEADS, 2, D_HALF)
    s = jnp.einsum('thcd,hckd->thck', q, sub_keys, preferred_element_type=jnp.float32)
    sv, si = lax.top_k(s, PEER_TOPK)
    cand = sv[:, :, 0, :, None] + sv[:, :, 1, None, :]
    cidx = si[:, :, 0, :, None] * N_KEYS + si[:, :, 1, None, :]
    cs, cpos = lax.top_k(cand.reshape(T, PEER_HEADS, PEER_TOPK * PEER_TOPK), PEER_TOPK)
    eidx = jnp.take_along_axis(cidx.reshape(T, PEER_HEADS, PEER_TOPK * PEER_TOPK), cpos, axis=-1)
    gate = jax.nn.softmax(cs, axis=-1).reshape(T, PEER_HEADS * PEER_TOPK)
    eflat = eidx.reshape(T, PEER_HEADS * PEER_TOPK)
    act_all = jnp.einsum('td,nd->tn', hf, u)
    act = jnp.take_along_axis(act_all, eflat, axis=-1).astype(jnp.float32)
    coef = gate * jax.nn.gelu(act, approximate=False)
    dense = jnp.zeros((T, N_EXPERTS), jnp.float32).at[jnp.arange(T)[:, None], eflat].add(coef)
    y = dense.astype(h.dtype) @ v
    return y.reshape(B, L, D)


def trunk(x, meta_tokens, ln1_g, w_in, conv_a_w, attn_sink, conf_dw_w, conf_dw_b, conf_ln_g, conf_ln_b,
          g_out_a, g_out_b, g_out_c, w_out, ln2_g, peer_wq, peer_subkeys, peer_u, peer_v, final_g):
    B = x.shape[0]
    meta = jnp.broadcast_to(meta_tokens.astype(x.dtype)[None], (B, N_META, D_MODEL))
    h = jnp.concatenate([meta, x], axis=1)
    for l in range(DEPTH):
        h = h + mixer_layer(rmsnorm(h, ln1_g[l]), w_in[l], conv_a_w[l], attn_sink[l], conf_dw_w[l],
                            conf_dw_b[l], conf_ln_g[l], conf_ln_b[l], g_out_a[l], g_out_b[l],
                            g_out_c[l], w_out[l])
        h = h + peer_layer(rmsnorm(h, ln2_g[l]), peer_wq[l], peer_subkeys[l], peer_u[l], peer_v[l])
    h = rmsnorm(h, final_g)
    return h[:, N_META:]


def setup_inputs(seed: int = 0) -> dict:
    key = jax.random.key(seed)
    ks = jax.random.split(key, 24)
    f32 = jnp.float32
    nrm = lambda k, shape, s: jax.random.normal(k, shape, f32) * s
    gain = lambda k, shape: 1.0 + 0.01 * jax.random.normal(k, shape, f32)
    return {
        "x_prompt": nrm(ks[0], (BATCH, SEQ, D_MODEL), 1.0),
        "x_sample": nrm(ks[1], (DEC_BATCH, DEC_SEQ, D_MODEL), 1.0),
        "meta_tokens": nrm(ks[2], (N_META, D_MODEL), 1.0),
        "ln1_g": gain(ks[3], (DEPTH, D_MODEL)),
        "w_in": nrm(ks[4], (DEPTH, D_MODEL, D_IN_PROJ), D_MODEL ** -0.5),
        "conv_a_w": nrm(ks[5], (DEPTH, SHORT_CONV_W, D_SCONV), SHORT_CONV_W ** -0.5),
        "attn_sink": nrm(ks[6], (DEPTH, N_Q_HEADS), 0.5),
        "conf_dw_w": nrm(ks[7], (DEPTH, CONF_CONV_W, D_CONF), CONF_CONV_W ** -0.5),
        "conf_dw_b": nrm(ks[8], (DEPTH, D_CONF), 0.01),
        "conf_ln_g": gain(ks[9], (DEPTH, D_CONF)),
        "conf_ln_b": nrm(ks[10], (DEPTH, D_CONF), 0.01),
        "g_out_a": gain(ks[11], (DEPTH, D_SCONV)),
        "g_out_b": gain(ks[12], (DEPTH, D_ATTN)),
        "g_out_c": gain(ks[13], (DEPTH, D_CONF)),
        "w_out": nrm(ks[14], (DEPTH, D_MIX, D_MODEL), D_MIX ** -0.5),
        "ln2_g": gain(ks[15], (DEPTH, D_MODEL)),
        "peer_wq": nrm(ks[16], (DEPTH, D_MODEL, PEER_HEADS * D_PKEY), D_MODEL ** -0.5),
        "peer_subkeys": nrm(ks[17], (DEPTH, PEER_HEADS, 2, N_KEYS, D_HALF), D_HALF ** -0.5),
        "peer_u": nrm(ks[18], (DEPTH, N_EXPERTS, D_MODEL), D_MODEL ** -0.5),
        "peer_v": nrm(ks[19], (DEPTH, N_EXPERTS, D_MODEL), (PEER_HEADS * PEER_TOPK) ** -0.5 * 4.0),
        "final_g": gain(ks[20], (D_MODEL,)),
    }


def reference(x_prompt, x_sample, meta_tokens, ln1_g, w_in, conv_a_w, attn_sink, conf_dw_w, conf_dw_b,
              conf_ln_g, conf_ln_b, g_out_a, g_out_b, g_out_c, w_out, ln2_g, peer_wq, peer_subkeys,
              peer_u, peer_v, final_g):
    y_prompt = trunk(x_prompt, meta_tokens, ln1_g, w_in, conv_a_w, attn_sink, conf_dw_w, conf_dw_b,
                     conf_ln_g, conf_ln_b, g_out_a, g_out_b, g_out_c, w_out, ln2_g, peer_wq,
                     peer_subkeys, peer_u, peer_v, final_g)
    y_sample = trunk(x_sample, meta_tokens, ln1_g, w_in, conv_a_w, attn_sink, conf_dw_w, conf_dw_b,
                     conf_ln_g, conf_ln_b, g_out_a, g_out_b, g_out_c, w_out, ln2_g, peer_wq,
                     peer_subkeys, peer_u, peer_v, final_g)
    return (y_prompt, y_sample)
```

```python
import functools

import numpy as np
import jax
import jax.numpy as jnp
from jax import lax
from jax.experimental import pallas as pl
from jax.experimental.pallas import tpu as pltpu

F32 = jnp.float32
BF16 = jnp.bfloat16

N_META = 16
D_SCONV = 512
N_Q_HEADS = 16
N_KV_HEADS = 4
HEAD_DIM = 64
GQA_GROUP = N_Q_HEADS // N_KV_HEADS
D_ATTN = N_Q_HEADS * HEAD_DIM
D_KV = N_KV_HEADS * HEAD_DIM
D_CONF = 512
SHORT_CONV_W = 3
CONF_CONV_W = 31
WINDOW = 128
BLK = 128
META_ROW0 = BLK - N_META
N_KEYS = 128
PEER_HEADS = 8
PEER_TOPK = 16
RMS_EPS = 1e-6
LN_EPS = 1e-5
NEG = -1e30
NEG_INF = float("-inf")
POS_INF = float("inf")

IN_TM = 256
PEER_TB = 512
PEER_NT = 512
VMEM_LIMIT = 56 * 1024 * 1024


def _rms(x, g):
    return x * lax.rsqrt(jnp.mean(x * x, axis=-1, keepdims=True) + RMS_EPS) * g


def _inproj_kernel(h_ref, g_ref, w_ref, ua_ref, bg_ref, q_ref, kv_ref, glu_ref):
    xn = _rms(h_ref[...], g_ref[...]).astype(BF16)
    z = jnp.dot(xn, w_ref[...], preferred_element_type=F32)
    o = 0
    a_x = z[:, o:o + D_SCONV]; o += D_SCONV
    a_bg = z[:, o:o + D_SCONV]; o += D_SCONV
    a_cg = z[:, o:o + D_SCONV]; o += D_SCONV
    q = z[:, o:o + D_ATTN]; o += D_ATTN
    kv = z[:, o:o + 2 * D_KV]; o += 2 * D_KV
    c_a = z[:, o:o + D_CONF]; o += D_CONF
    c_g = z[:, o:o + D_CONF]
    ua_ref[...] = a_cg * a_x
    bg_ref[...] = a_bg
    q_ref[...] = (q * (HEAD_DIM ** -0.5)).astype(BF16)
    kv_ref[...] = kv.astype(BF16)
    glu_ref[...] = c_a * jax.nn.sigmoid(c_g)


def _inproj(h, g, w_bf):
    tp, d = h.shape
    n = w_bf.shape[1]
    row = lambda w: pl.BlockSpec((IN_TM, w), lambda i: (i, 0))
    return pl.pallas_call(
        _inproj_kernel,
        grid=(tp // IN_TM,),
        in_specs=[row(d),
                  pl.BlockSpec((1, d), lambda i: (0, 0)),
                  pl.BlockSpec((d, n), lambda i: (0, 0), pipeline_mode=pl.Buffered(1))],
        out_specs=[row(D_SCONV), row(D_SCONV), row(D_ATTN), row(2 * D_KV), row(D_CONF)],
        out_shape=[jax.ShapeDtypeStruct((tp, D_SCONV), F32),
                   jax.ShapeDtypeStruct((tp, D_SCONV), F32),
                   jax.ShapeDtypeStruct((tp, D_ATTN), BF16),
                   jax.ShapeDtypeStruct((tp, 2 * D_KV), BF16),
                   jax.ShapeDtypeStruct((tp, D_CONF), F32)],
        compiler_params=pltpu.CompilerParams(dimension_semantics=("arbitrary",),
                                             vmem_limit_bytes=VMEM_LIMIT),
    )(h, g, w_bf)


def _mixer_kernel(jpos_ref, nbs_ref, s0_ref, fvr_ref,
                  ua_p, ua_c, ua_n, bg_ref, q_ref, kv_p, kv_c, kv_n, kv_m, glu_p, glu_c, glu_n, h_ref,
                  cw_ref, sink_ref, dw_ref, db_ref, lg_ref, lb_ref, ga_ref, gb_ref, gc_ref, wo_ref,
                  out_ref, xa_ref, xc_ref, yb_ref):
    i = pl.program_id(0)
    nblk = pl.num_programs(0)
    j = jpos_ref[i]
    nb = nbs_ref[i]
    has_prev = (i > 0).astype(F32)
    has_next = (i < nblk - 1).astype(F32)

    xa_ref[0:8, :] = ua_p[...] * has_prev
    xa_ref[8:8 + BLK, :] = ua_c[...]
    xa_ref[8 + BLK:16 + BLK, :] = ua_n[...] * has_next
    conv_a = jnp.zeros((BLK, D_SCONV), F32)
    for k in range(SHORT_CONV_W):
        conv_a = conv_a + cw_ref[k:k + 1, :] * xa_ref[pl.ds(8 + k - SHORT_CONV_W // 2, BLK), :]
    y_a = _rms(bg_ref[...] * conv_a, ga_ref[...])

    xc_ref[0:16, :] = glu_p[...] * has_prev
    xc_ref[16:16 + BLK, :] = glu_c[...]
    xc_ref[16 + BLK:32 + BLK, :] = glu_n[...] * has_next
    conv_c = jnp.zeros((BLK, D_CONF), F32)
    for k in range(CONF_CONV_W):
        conv_c = conv_c + dw_ref[k:k + 1, :] * xc_ref[pl.ds(16 + k - CONF_CONV_W // 2, BLK), :]
    conv_c = conv_c + db_ref[...]
    mu = jnp.mean(conv_c, axis=-1, keepdims=True)
    xc = conv_c - mu
    var = jnp.mean(xc * xc, axis=-1, keepdims=True)
    ln = xc * lax.rsqrt(var + LN_EPS) * lg_ref[...] + lb_ref[...]
    y_c = _rms(ln * jax.nn.sigmoid(ln), gc_ref[...])

    rq = lax.broadcasted_iota(jnp.int32, (BLK, BLK), 0)
    rk = lax.broadcasted_iota(jnp.int32, (BLK, BLK), 1)
    jb = jnp.full((BLK, BLK), j, jnp.int32)
    ok_p = jnp.logical_and(rk >= rq, jb >= 2)
    ok_c = jb >= 1
    ok_n = jnp.logical_and(rk <= rq, jb <= nb - 2)
    d_p = (rq + BLK - rk).astype(F32)
    d_c = jnp.abs(rq - rk).astype(F32)
    d_n = (BLK + rk - rq).astype(F32)
    qpos = j * BLK - META_ROW0 + lax.broadcasted_iota(jnp.int32, (BLK, N_META), 0)
    mpos = lax.broadcasted_iota(jnp.int32, (BLK, N_META), 1)
    d_m = jnp.minimum(jnp.abs(qpos - mpos), WINDOW).astype(F32)
    nt = (((1,), (1,)), ((), ()))
    for hd in range(N_Q_HEADS):
        g = hd // GQA_GROUP
        slope = float(2.0 ** (-8.0 * (hd + 1) / N_Q_HEADS))
        ks = slice(g * HEAD_DIM, (g + 1) * HEAD_DIM)
        vs = slice(D_KV + g * HEAD_DIM, D_KV + (g + 1) * HEAD_DIM)
        qh = q_ref[:, hd * HEAD_DIM:(hd + 1) * HEAD_DIM]
        s_p = lax.dot_general(qh, kv_p[:, ks], nt, preferred_element_type=F32)
        s_c = lax.dot_general(qh, kv_c[:, ks], nt, preferred_element_type=F32)
        s_n = lax.dot_general(qh, kv_n[:, ks], nt, preferred_element_type=F32)
        s_m = lax.dot_general(qh, kv_m[:, ks], nt, preferred_element_type=F32)
        s_p = jnp.where(ok_p, s_p - slope * d_p, NEG)
        s_c = jnp.where(ok_c, s_c - slope * d_c, NEG)
        s_n = jnp.where(ok_n, s_n - slope * d_n, NEG)
        s_m = s_m - slope * d_m
        sink = sink_ref[hd]
        mx = jnp.maximum(jnp.maximum(jnp.max(s_p, axis=-1, keepdims=True), jnp.max(s_c, axis=-1, keepdims=True)),
                         jnp.maximum(jnp.max(s_n, axis=-1, keepdims=True), jnp.max(s_m, axis=-1, keepdims=True)))
        mx = jnp.maximum(mx, sink)
        p_p = jnp.exp(s_p - mx)
        p_c = jnp.exp(s_c - mx)
        p_n = jnp.exp(s_n - mx)
        p_m = jnp.exp(s_m - mx)
        den = (jnp.sum(p_p, axis=-1, keepdims=True) + jnp.sum(p_c, axis=-1, keepdims=True)
               + jnp.sum(p_n, axis=-1, keepdims=True) + jnp.sum(p_m, axis=-1, keepdims=True)
               + jnp.exp(sink - mx))
        o = (jnp.dot(p_p.astype(BF16), kv_p[:, vs], preferred_element_type=F32)
             + jnp.dot(p_c.astype(BF16), kv_c[:, vs], preferred_element_type=F32)
             + jnp.dot(p_n.astype(BF16), kv_n[:, vs], preferred_element_type=F32)
             + jnp.dot(p_m.astype(BF16), kv_m[:, vs], preferred_element_type=F32))
        yb_ref[:, hd * HEAD_DIM:(hd + 1) * HEAD_DIM] = o / den
    y_b = _rms(yb_ref[...], gb_ref[...])

    out = (jnp.dot(y_a.astype(BF16), wo_ref[0:D_SCONV, :], preferred_element_type=F32)
           + jnp.dot(y_b.astype(BF16), wo_ref[D_SCONV:D_SCONV + D_ATTN, :], preferred_element_type=F32)
           + jnp.dot(y_c.astype(BF16), wo_ref[D_SCONV + D_ATTN:, :], preferred_element_type=F32))
    keep = lax.broadcasted_iota(jnp.int32, (BLK, 1), 0) >= fvr_ref[i]
    out_ref[...] = jnp.where(keep, h_ref[...] + out, 0.0)


def _mixer(sched, ua, bg, q, kv, glu, h, cw, sink, dw, db, lg, lb, ga, gb, gc, wo_bf):
    tp, d = h.shape
    nblk = tp // BLK
    jpos, nbs, s0, fvr = sched
    cur = lambda w: pl.BlockSpec((BLK, w), lambda i, *_: (i, 0))
    prv = lambda w: pl.BlockSpec((BLK, w), lambda i, *_: (jnp.maximum(i - 1, 0), 0))
    nxt = lambda w: pl.BlockSpec((BLK, w), lambda i, *_: (jnp.minimum(i + 1, nblk - 1), 0))
    halo_p = lambda r, w: pl.BlockSpec((r, w), lambda i, *_: (jnp.maximum(i * (BLK // r) - 1, 0), 0))
    halo_n = lambda r, w: pl.BlockSpec((r, w), lambda i, *_: (jnp.minimum((i + 1) * (BLK // r), nblk * (BLK // r) - 1), 0))
    meta = pl.BlockSpec((N_META, 2 * D_KV), lambda i, jp, nb, s0r, fv: (s0r[i] * (BLK // N_META) + BLK // N_META - 1, 0))
    full = lambda a: pl.BlockSpec(a.shape, lambda i, *_: (0,) * a.ndim)
    grid_spec = pltpu.PrefetchScalarGridSpec(
        num_scalar_prefetch=4,
        grid=(nblk,),
        in_specs=[halo_p(8, D_SCONV), cur(D_SCONV), halo_n(8, D_SCONV), cur(D_SCONV), cur(D_ATTN),
                  prv(2 * D_KV), cur(2 * D_KV), nxt(2 * D_KV), meta,
                  halo_p(16, D_CONF), cur(D_CONF), halo_n(16, D_CONF), cur(d),
                  full(cw), pl.BlockSpec(memory_space=pltpu.SMEM), full(dw), full(db), full(lg), full(lb),
                  full(ga), full(gb), full(gc),
                  pl.BlockSpec(wo_bf.shape, lambda i, *_: (0, 0), pipeline_mode=pl.Buffered(1))],
        out_specs=cur(d),
        scratch_shapes=[pltpu.VMEM((BLK + 16, D_SCONV), F32),
                        pltpu.VMEM((BLK + 32, D_CONF), F32),
                        pltpu.VMEM((BLK, D_ATTN), F32)],
    )
    return pl.pallas_call(
        _mixer_kernel,
        grid_spec=grid_spec,
        out_shape=jax.ShapeDtypeStruct((tp, d), F32),
        compiler_params=pltpu.CompilerParams(dimension_semantics=("arbitrary",),
                                             vmem_limit_bytes=VMEM_LIMIT),
    )(jpos, nbs, s0, fvr, ua, ua, ua, bg, q, kv, kv, kv, kv, glu, glu, glu, h,
      cw, sink, dw, db, lg, lb, ga, gb, gc, wo_bf)


def _peer_kernel(h_ref, g_ref, wqt_ref, sk_ref, u_ref, vt_ref, fg_ref, out_ref,
                 hnt_ref, qt_ref, s_ref, top_ref, thr_ref, act_ref, d_ref, yt_ref, *, final_norm):
    e = pl.program_id(1)
    n_e = pl.num_programs(1)
    tb = h_ref.shape[0]
    n_tc = tb // 128

    @pl.when(e == 0)
    def _route():
        xn = _rms(h_ref[...], g_ref[...])
        hnt_ref[...] = xn.T.astype(BF16)
        qt_ref[...] = jnp.dot(wqt_ref[...], hnt_ref[...], preferred_element_type=F32).astype(BF16)
        yt_ref[...] = jnp.zeros_like(yt_ref)

        def score_body(hc, carry):
            row0 = pl.multiple_of(hc * 128, 128)
            s_ref[hc] = jnp.dot(sk_ref[hc], qt_ref[pl.ds(row0, 128), :], preferred_element_type=F32)
            return carry
        lax.fori_loop(0, 2 * PEER_HEADS, score_body, 0)

        def top_body(it, carry):
            hc = it // n_tc
            col0 = pl.multiple_of((it % n_tc) * 128, 128)
            cur = s_ref[hc, :, pl.ds(col0, 128)]
            for k in range(PEER_TOPK):
                m = jnp.max(cur, axis=0, keepdims=True)
                top_ref[hc, k:k + 1, pl.ds(col0, 128)] = m
                cur = jnp.where(cur == m, NEG_INF, cur)
            return carry
        lax.fori_loop(0, 2 * PEER_HEADS * n_tc, top_body, 0)

        def cand_body(it, carry):
            hh = it // n_tc
            col0 = pl.multiple_of((it % n_tc) * 128, 128)
            cols = pl.ds(col0, 128)
            b = top_ref[2 * hh + 1, :, cols]
            m = top_ref[2 * hh, 0:1, cols] + top_ref[2 * hh + 1, 0:1, cols]
            am = [top_ref[2 * hh, k:k + 1, cols] - m for k in range(PEER_TOPK)]
            cands = [a + b for a in am]
            cur = list(cands)
            thr = None
            for k in range(PEER_TOPK):
                mx = cur[0]
                for c in cur[1:]:
                    mx = jnp.maximum(mx, c)
                mx = jnp.max(mx, axis=0, keepdims=True)
                if k == PEER_TOPK - 1:
                    thr = mx
                else:
                    cur = [jnp.where(c == mx, NEG_INF, c) for c in cur]
            z = jnp.zeros((PEER_TOPK, 128), F32)
            for c in cands:
                z = z + jnp.where(c >= thr, jnp.exp(c), 0.0)
            logz = jnp.log(jnp.sum(z, axis=0, keepdims=True))
            tz = jnp.full((PEER_TOPK, 128), POS_INF, F32)
            for a, c in zip(am, cands):
                tz = jnp.minimum(tz, jnp.where(c >= thr, (a - logz) + b, POS_INF))
            tz = jnp.min(tz, axis=0, keepdims=True)
            thr_ref[hh, :, cols] = jnp.broadcast_to(tz, (8, 128))
            s_ref[2 * hh, :, cols] = (s_ref[2 * hh, :, cols] - m) - logz
            return carry
        lax.fori_loop(0, PEER_HEADS * n_tc, cand_body, 0)

    nt = u_ref.shape[0]
    n_r = nt // 128
    act_ref[...] = jnp.dot(u_ref[...], hnt_ref[...], preferred_element_type=F32)

    def chunk_body(it, carry):
        r = it // n_tc
        row0 = pl.multiple_of(r * 128, 128)
        col0 = pl.multiple_of((it % n_tc) * 128, 128)
        cols = pl.ds(col0, 128)
        i1 = e * n_r + r
        grp0 = pl.multiple_of((i1 // 8) * 8, 8)
        in_grp = lax.broadcasted_iota(jnp.int32, (8, 128), 0) == i1 % 8
        w = jnp.zeros((128, 128), F32)
        for hh in range(PEER_HEADS):
            s1_row = jnp.sum(jnp.where(in_grp, s_ref[2 * hh, pl.ds(grp0, 8), cols], 0.0), axis=0, keepdims=True)
            x = s1_row + s_ref[2 * hh + 1, :, cols]
            thr = jnp.broadcast_to(thr_ref[hh, 0:1, cols], (128, 128))
            w = w + jnp.where(x >= thr, jnp.exp(x), 0.0)
        a = act_ref[pl.ds(row0, 128), cols]
        gl = 0.5 * a * (1.0 + lax.erf(a * (2.0 ** -0.5)))
        d_ref[pl.ds(row0, 128), cols] = (gl * w).astype(BF16)
        return carry
    lax.fori_loop(0, n_r * n_tc, chunk_body, 0)
    yt_ref[...] += jnp.dot(vt_ref[...], d_ref[...], preferred_element_type=F32)

    @pl.when(e == n_e - 1)
    def _finish():
        y = h_ref[...] + yt_ref[...].T
        if final_norm:
            y = _rms(y, fg_ref[...])
        out_ref[...] = y


def _peer(h, g, wqt_bf, sk_bf, u_bf, vt_bf, fg, final_norm):
    tp, d = h.shape
    ne = u_bf.shape[0]
    tb, nt = PEER_TB, PEER_NT
    return pl.pallas_call(
        functools.partial(_peer_kernel, final_norm=final_norm),
        grid=(tp // tb, ne // nt),
        in_specs=[pl.BlockSpec((tb, d), lambda t, e: (t, 0)),
                  pl.BlockSpec((1, d), lambda t, e: (0, 0)),
                  pl.BlockSpec(wqt_bf.shape, lambda t, e: (0, 0), pipeline_mode=pl.Buffered(1)),
                  pl.BlockSpec(sk_bf.shape, lambda t, e: (0, 0, 0)),
                  pl.BlockSpec((nt, d), lambda t, e: (e, 0)),
                  pl.BlockSpec((d, nt), lambda t, e: (0, e)),
                  pl.BlockSpec((1, d), lambda t, e: (0, 0))],
        out_specs=pl.BlockSpec((tb, d), lambda t, e: (t, 0)),
        out_shape=jax.ShapeDtypeStruct((tp, d), F32),
        scratch_shapes=[pltpu.VMEM((d, tb), BF16),
                        pltpu.VMEM((wqt_bf.shape[0], tb), BF16),
                        pltpu.VMEM((2 * PEER_HEADS, N_KEYS, tb), F32),
                        pltpu.VMEM((2 * PEER_HEADS, PEER_TOPK, tb), F32),
                        pltpu.VMEM((PEER_HEADS, 8, tb), F32),
                        pltpu.VMEM((nt, tb), F32),
                        pltpu.VMEM((nt, tb), BF16),
                        pltpu.VMEM((d, tb), F32)],
        compiler_params=pltpu.CompilerParams(dimension_semantics=("arbitrary", "arbitrary"),
                                             vmem_limit_bytes=VMEM_LIMIT),
    )(h, g, wqt_bf, sk_bf, u_bf, vt_bf, fg)


def _schedule(seq_lens):
    jpos, nbs, s0, fvr = [], [], [], []
    start = 0
    for s in seq_lens:
        nb = s // BLK + 1
        for j in range(nb):
            jpos.append(j); nbs.append(nb); s0.append(start); fvr.append(META_ROW0 if j == 0 else 0)
        start += nb
    per = max(PEER_TB, IN_TM) // BLK
    while start % per:
        jpos.append(0); nbs.append(1); s0.append(start); fvr.append(BLK)
        start += 1
    return tuple(jnp.asarray(np.asarray(a, np.int32)) for a in (jpos, nbs, s0, fvr)), start


def kernel(x_prompt, x_sample, meta_tokens, ln1_g, w_in, conv_a_w, attn_sink, conf_dw_w, conf_dw_b, conf_ln_g,
           conf_ln_b, g_out_a, g_out_b, g_out_c, w_out, ln2_g, peer_wq, peer_subkeys, peer_u, peer_v, final_g):
    d = x_prompt.shape[-1]
    depth = w_in.shape[0]
    seqs = [x_prompt[b] for b in range(x_prompt.shape[0])] + [x_sample[b] for b in range(x_sample.shape[0])]
    seq_lens = [s.shape[0] for s in seqs]
    assert all(s % BLK == 0 for s in seq_lens)
    sched, nblk = _schedule(seq_lens)
    lead = jnp.concatenate([jnp.zeros((META_ROW0, d), F32), meta_tokens.astype(F32)], axis=0)
    parts = []
    for s in seqs:
        parts += [lead, s]
    n_rows = sum(seq_lens) + BLK * len(seqs)
    if nblk * BLK > n_rows:
        parts.append(jnp.zeros((nblk * BLK - n_rows, d), F32))
    h = jnp.concatenate(parts, axis=0)

    row = lambda a: a.reshape(1, -1)
    for l in range(depth):
        ua, bg, q, kv, glu = _inproj(h, row(ln1_g[l]), w_in[l].astype(BF16))
        h = _mixer(sched, ua, bg, q, kv, glu, h, conv_a_w[l], attn_sink[l], conf_dw_w[l], row(conf_dw_b[l]),
                   row(conf_ln_g[l]), row(conf_ln_b[l]), row(g_out_a[l]), row(g_out_b[l]), row(g_out_c[l]),
                   w_out[l].astype(BF16))
        sk = peer_subkeys[l].reshape(2 * PEER_HEADS, N_KEYS, -1).astype(BF16)
        h = _peer(h, row(ln2_g[l]), peer_wq[l].T.astype(BF16), sk, peer_u[l].astype(BF16),
                  peer_v[l].T.astype(BF16), row(final_g), final_norm=(l == depth - 1))

    outs = []
    off = 0
    for s in seq_lens:
        outs.append(h[off + BLK:off + BLK + s])
        off += BLK + s
    nbp = x_prompt.shape[0]
    return (jnp.stack(outs[:nbp], axis=0), jnp.stack(outs[nbp:], axis=0))
```

```python
import functools

import numpy as np
import jax
import jax.numpy as jnp
from jax import lax
from jax.experimental import pallas as pl
from jax.experimental.pallas import tpu as pltpu

F32 = jnp.float32
BF16 = jnp.bfloat16

N_META = 16
D_SCONV = 512
N_Q_HEADS = 16
N_KV_HEADS = 4
HEAD_DIM = 64
GQA_GROUP = N_Q_HEADS // N_KV_HEADS
D_ATTN = N_Q_HEADS * HEAD_DIM
D_KV = N_KV_HEADS * HEAD_DIM
D_CONF = 512
SHORT_CONV_W = 3
CONF_CONV_W = 31
WINDOW = 128
BLK = 128
META_ROW0 = BLK - N_META
N_KEYS = 128
PEER_HEADS = 8
PEER_TOPK = 16
RMS_EPS = 1e-6
LN_EPS = 1e-5
NEG = -1e30
NEG_INF = float("-inf")
POS_INF = float("inf")
LOG2E = 1.4426950408889634

IN_TM = 256
PEER_TB = 512
PEER_NT = 512
MXU_N = 256
Y_ROWS = 512
D_SUB = 64
VMEM_LIMIT = 56 * 1024 * 1024


def _rms(x, g):
    return x * lax.rsqrt(jnp.mean(x * x, axis=-1, keepdims=True) + RMS_EPS) * g


def _inproj_kernel(h_ref, g_ref, w_ref, ua_ref, bg_ref, q_ref, kv_ref, glu_ref):
    xn = _rms(h_ref[...], g_ref[...]).astype(BF16)
    z = jnp.dot(xn, w_ref[...], preferred_element_type=F32)
    o = 0
    a_x = z[:, o:o + D_SCONV]; o += D_SCONV
    a_bg = z[:, o:o + D_SCONV]; o += D_SCONV
    a_cg = z[:, o:o + D_SCONV]; o += D_SCONV
    q = z[:, o:o + D_ATTN]; o += D_ATTN
    kv = z[:, o:o + 2 * D_KV]; o += 2 * D_KV
    c_a = z[:, o:o + D_CONF]; o += D_CONF
    c_g = z[:, o:o + D_CONF]
    ua_ref[...] = a_cg * a_x
    bg_ref[...] = a_bg
    q_ref[...] = (q * (HEAD_DIM ** -0.5)).astype(BF16)
    kv_ref[...] = kv.astype(BF16)
    glu_ref[...] = c_a * jax.nn.sigmoid(c_g)


def _inproj(h, g, w_bf):
    tp, d = h.shape
    n = w_bf.shape[1]
    row = lambda w: pl.BlockSpec((IN_TM, w), lambda i: (i, 0))
    return pl.pallas_call(
        _inproj_kernel,
        grid=(tp // IN_TM,),
        in_specs=[row(d),
                  pl.BlockSpec((1, d), lambda i: (0, 0)),
                  pl.BlockSpec((d, n), lambda i: (0, 0), pipeline_mode=pl.Buffered(1))],
        out_specs=[row(D_SCONV), row(D_SCONV), row(D_ATTN), row(2 * D_KV), row(D_CONF)],
        out_shape=[jax.ShapeDtypeStruct((tp, D_SCONV), F32),
                   jax.ShapeDtypeStruct((tp, D_SCONV), F32),
                   jax.ShapeDtypeStruct((tp, D_ATTN), BF16),
                   jax.ShapeDtypeStruct((tp, 2 * D_KV), BF16),
                   jax.ShapeDtypeStruct((tp, D_CONF), F32)],
        compiler_params=pltpu.CompilerParams(dimension_semantics=("arbitrary",),
                                             vmem_limit_bytes=VMEM_LIMIT),
    )(h, g, w_bf)


def _mixer_kernel(jpos_ref, nbs_ref, s0_ref, fvr_ref,
                  ua_p, ua_c, ua_n, bg_ref, q_ref, kv_p, kv_c, kv_n, kv_m, glu_p, glu_c, glu_n, h_ref,
                  cw_ref, sink_ref, dw_ref, db_ref, lg_ref, lb_ref, ga_ref, gb_ref, gc_ref, wo_ref,
                  out_ref, xa_ref, xc_ref, yb_ref):
    i = pl.program_id(0)
    nblk = pl.num_programs(0)
    j = jpos_ref[i]
    nb = nbs_ref[i]
    has_prev = (i > 0).astype(F32)
    has_next = (i < nblk - 1).astype(F32)

    xa_ref[0:8, :] = ua_p[...] * has_prev
    xa_ref[8:8 + BLK, :] = ua_c[...]
    xa_ref[8 + BLK:16 + BLK, :] = ua_n[...] * has_next
    conv_a = jnp.zeros((BLK, D_SCONV), F32)
    for k in range(SHORT_CONV_W):
        conv_a = conv_a + cw_ref[k:k + 1, :] * xa_ref[pl.ds(8 + k - SHORT_CONV_W // 2, BLK), :]
    y_a = _rms(bg_ref[...] * conv_a, ga_ref[...])

    xc_ref[0:16, :] = glu_p[...] * has_prev
    xc_ref[16:16 + BLK, :] = glu_c[...]
    xc_ref[16 + BLK:32 + BLK, :] = glu_n[...] * has_next
    conv_c = jnp.zeros((BLK, D_CONF), F32)
    for k in range(CONF_CONV_W):
        conv_c = conv_c + dw_ref[k:k + 1, :] * xc_ref[pl.ds(16 + k - CONF_CONV_W // 2, BLK), :]
    conv_c = conv_c + db_ref[...]
    mu = jnp.mean(conv_c, axis=-1, keepdims=True)
    xc = conv_c - mu
    var = jnp.mean(xc * xc, axis=-1, keepdims=True)
    ln = xc * lax.rsqrt(var + LN_EPS) * lg_ref[...] + lb_ref[...]
    y_c = _rms(ln * jax.nn.sigmoid(ln), gc_ref[...])

    rq = lax.broadcasted_iota(jnp.int32, (BLK, BLK), 0)
    rk = lax.broadcasted_iota(jnp.int32, (BLK, BLK), 1)
    jb = jnp.full((BLK, BLK), j, jnp.int32)
    ok_p = jnp.logical_and(rk >= rq, jb >= 2)
    ok_c = jb >= 1
    ok_n = jnp.logical_and(rk <= rq, jb <= nb - 2)
    d_p = (rq + BLK - rk).astype(F32)
    d_c = jnp.abs(rq - rk).astype(F32)
    d_n = (BLK + rk - rq).astype(F32)
    qpos = j * BLK - META_ROW0 + lax.broadcasted_iota(jnp.int32, (BLK, N_META), 0)
    mpos = lax.broadcasted_iota(jnp.int32, (BLK, N_META), 1)
    d_m = jnp.minimum(jnp.abs(qpos - mpos), WINDOW).astype(F32)
    nt = (((1,), (1,)), ((), ()))
    for hd in range(N_Q_HEADS):
        g = hd // GQA_GROUP
        slope = float(2.0 ** (-8.0 * (hd + 1) / N_Q_HEADS))
        ks = slice(g * HEAD_DIM, (g + 1) * HEAD_DIM)
        vs = slice(D_KV + g * HEAD_DIM, D_KV + (g + 1) * HEAD_DIM)
        qh = q_ref[:, hd * HEAD_DIM:(hd + 1) * HEAD_DIM]
        s_p = lax.dot_general(qh, kv_p[:, ks], nt, preferred_element_type=F32)
        s_c = lax.dot_general(qh, kv_c[:, ks], nt, preferred_element_type=F32)
        s_n = lax.dot_general(qh, kv_n[:, ks], nt, preferred_element_type=F32)
        s_m = lax.dot_general(qh, kv_m[:, ks], nt, preferred_element_type=F32)
        s_p = jnp.where(ok_p, s_p - slope * d_p, NEG)
        s_c = jnp.where(ok_c, s_c - slope * d_c, NEG)
        s_n = jnp.where(ok_n, s_n - slope * d_n, NEG)
        s_m = s_m - slope * d_m
        sink = sink_ref[hd]
        mx = jnp.maximum(jnp.maximum(jnp.max(s_p, axis=-1, keepdims=True), jnp.max(s_c, axis=-1, keepdims=True)),
                         jnp.maximum(jnp.max(s_n, axis=-1, keepdims=True), jnp.max(s_m, axis=-1, keepdims=True)))
        mx = jnp.maximum(mx, sink)
        p_p = jnp.exp(s_p - mx)
        p_c = jnp.exp(s_c - mx)
        p_n = jnp.exp(s_n - mx)
        p_m = jnp.exp(s_m - mx)
        den = (jnp.sum(p_p, axis=-1, keepdims=True) + jnp.sum(p_c, axis=-1, keepdims=True)
               + jnp.sum(p_n, axis=-1, keepdims=True) + jnp.sum(p_m, axis=-1, keepdims=True)
               + jnp.exp(sink - mx))
        o = (jnp.dot(p_p.astype(BF16), kv_p[:, vs], preferred_element_type=F32)
             + jnp.dot(p_c.astype(BF16), kv_c[:, vs], preferred_element_type=F32)
             + jnp.dot(p_n.astype(BF16), kv_n[:, vs], preferred_element_type=F32)
             + jnp.dot(p_m.astype(BF16), kv_m[:, vs], preferred_element_type=F32))
        yb_ref[:, hd * HEAD_DIM:(hd + 1) * HEAD_DIM] = o / den
    y_b = _rms(yb_ref[...], gb_ref[...])

    out = (jnp.dot(y_a.astype(BF16), wo_ref[0:D_SCONV, :], preferred_element_type=F32)
           + jnp.dot(y_b.astype(BF16), wo_ref[D_SCONV:D_SCONV + D_ATTN, :], preferred_element_type=F32)
           + jnp.dot(y_c.astype(BF16), wo_ref[D_SCONV + D_ATTN:, :], preferred_element_type=F32))
    keep = lax.broadcasted_iota(jnp.int32, (BLK, 1), 0) >= fvr_ref[i]
    out_ref[...] = jnp.where(keep, h_ref[...] + out, 0.0)


def _mixer(sched, ua, bg, q, kv, glu, h, cw, sink, dw, db, lg, lb, ga, gb, gc, wo_bf):
    tp, d = h.shape
    nblk = tp // BLK
    jpos, nbs, s0, fvr = sched
    cur = lambda w: pl.BlockSpec((BLK, w), lambda i, *_: (i, 0))
    prv = lambda w: pl.BlockSpec((BLK, w), lambda i, *_: (jnp.maximum(i - 1, 0), 0))
    nxt = lambda w: pl.BlockSpec((BLK, w), lambda i, *_: (jnp.minimum(i + 1, nblk - 1), 0))
    halo_p = lambda r, w: pl.BlockSpec((r, w), lambda i, *_: (jnp.maximum(i * (BLK // r) - 1, 0), 0))
    halo_n = lambda r, w: pl.BlockSpec((r, w), lambda i, *_: (jnp.minimum((i + 1) * (BLK // r), nblk * (BLK // r) - 1), 0))
    meta = pl.BlockSpec((N_META, 2 * D_KV), lambda i, jp, nb, s0r, fv: (s0r[i] * (BLK // N_META) + BLK // N_META - 1, 0))
    full = lambda a: pl.BlockSpec(a.shape, lambda i, *_: (0,) * a.ndim)
    grid_spec = pltpu.PrefetchScalarGridSpec(
        num_scalar_prefetch=4,
        grid=(nblk,),
        in_specs=[halo_p(8, D_SCONV), cur(D_SCONV), halo_n(8, D_SCONV), cur(D_SCONV), cur(D_ATTN),
                  prv(2 * D_KV), cur(2 * D_KV), nxt(2 * D_KV), meta,
                  halo_p(16, D_CONF), cur(D_CONF), halo_n(16, D_CONF), cur(d),
                  full(cw), pl.BlockSpec(memory_space=pltpu.SMEM), full(dw), full(db), full(lg), full(lb),
                  full(ga), full(gb), full(gc),
                  pl.BlockSpec(wo_bf.shape, lambda i, *_: (0, 0), pipeline_mode=pl.Buffered(1))],
        out_specs=cur(d),
        scratch_shapes=[pltpu.VMEM((BLK + 16, D_SCONV), F32),
                        pltpu.VMEM((BLK + 32, D_CONF), F32),
                        pltpu.VMEM((BLK, D_ATTN), F32)],
    )
    return pl.pallas_call(
        _mixer_kernel,
        grid_spec=grid_spec,
        out_shape=jax.ShapeDtypeStruct((tp, d), F32),
        compiler_params=pltpu.CompilerParams(dimension_semantics=("arbitrary",),
                                             vmem_limit_bytes=VMEM_LIMIT),
    )(jpos, nbs, s0, fvr, ua, ua, ua, bg, q, kv, kv, kv, kv, glu, glu, glu, h,
      cw, sink, dw, db, lg, lb, ga, gb, gc, wo_bf)


def _peer_route(h_ref, g_ref, wqt_ref, sk_ref, hnt_ref, qt_ref, s_ref, top_ref, thr_ref):
    tb = h_ref.shape[0]
    n_tc = tb // 128
    xn = _rms(h_ref[...], g_ref[...])
    hnt_ref[...] = xn.T.astype(BF16)
    qt_ref[...] = jnp.dot(wqt_ref[...], hnt_ref[...], preferred_element_type=F32).astype(BF16)

    def score_body(hc, carry):
        row0 = pl.multiple_of(hc * 128, 128)
        s_ref[hc] = jnp.dot(sk_ref[hc], qt_ref[pl.ds(row0, 128), :], preferred_element_type=F32) * LOG2E
        return carry
    lax.fori_loop(0, 2 * PEER_HEADS, score_body, 0)

    def top_body(it, carry):
        hc = it // n_tc
        col0 = pl.multiple_of((it % n_tc) * 128, 128)
        cur = s_ref[hc, :, pl.ds(col0, 128)]
        for k in range(PEER_TOPK):
            m = jnp.max(cur, axis=0, keepdims=True)
            top_ref[hc, k:k + 1, pl.ds(col0, 128)] = m
            cur = jnp.where(cur == m, NEG_INF, cur)
        return carry
    lax.fori_loop(0, 2 * PEER_HEADS * n_tc, top_body, 0)

    def cand_body(it, carry):
        hh = it // n_tc
        col0 = pl.multiple_of((it % n_tc) * 128, 128)
        cols = pl.ds(col0, 128)
        b = top_ref[2 * hh + 1, :, cols]
        m = top_ref[2 * hh, 0:1, cols] + top_ref[2 * hh + 1, 0:1, cols]
        am = [top_ref[2 * hh, k:k + 1, cols] - m for k in range(PEER_TOPK)]
        cands = [a + b for a in am]
        cur = list(cands)
        thr = None
        for k in range(PEER_TOPK):
            mx = cur[0]
            for c in cur[1:]:
                mx = jnp.maximum(mx, c)
            mx = jnp.max(mx, axis=0, keepdims=True)
            if k == PEER_TOPK - 1:
                thr = mx
            else:
                cur = [jnp.where(c == mx, NEG_INF, c) for c in cur]
        z = jnp.zeros((PEER_TOPK, 128), F32)
        for c in cands:
            z = z + jnp.where(c >= thr, jnp.exp2(c), 0.0)
        logz = jnp.log2(jnp.sum(z, axis=0, keepdims=True))
        tz = jnp.full((PEER_TOPK, 128), POS_INF, F32)
        for a, c in zip(am, cands):
            tz = jnp.minimum(tz, jnp.where(c >= thr, (a - logz) + b, POS_INF))
        tz = jnp.min(tz, axis=0, keepdims=True)
        thr_ref[hh, :, cols] = jnp.broadcast_to(tz, (8, 128))
        s_ref[2 * hh, :, cols] = (s_ref[2 * hh, :, cols] - m) - logz
        return carry
    lax.fori_loop(0, PEER_HEADS * n_tc, cand_body, 0)


def _peer_kernel(h_ref, g_ref, wqt_ref, sk_ref, u_ref, vt_ref, fg_ref, out_ref,
                 hnt_ref, qt_ref, s_ref, top_ref, thr_ref, s1b_ref, act0_ref, act1_ref, d0_ref, d1_ref, yt_ref,
                 *, final_norm, n_e):
    e = pl.program_id(1)
    tb = h_ref.shape[0]
    n_tc = tb // 128
    nt = u_ref.shape[0]
    n_r = nt // 128

    @pl.when(e == 0)
    def _prologue():
        _peer_route(h_ref, g_ref, wqt_ref, sk_ref, hnt_ref, qt_ref, s_ref, top_ref, thr_ref)
        act0_ref[...] = jnp.dot(u_ref[...], hnt_ref[...], preferred_element_type=F32)
        d1_ref[...] = jnp.zeros_like(d1_ref)
        yt_ref[...] = jnp.zeros_like(yt_ref)

    def body(act_in, act_out, d_in, d_out):
        d_model = vt_ref.shape[0]
        n_q = d_model // Y_ROWS
        k_q = d_model // n_q

        i1_0 = jnp.minimum((e - 1) * n_r, N_KEYS - n_r)
        grp0 = pl.multiple_of((i1_0 // 8) * 8, 8)
        sub_iota = lax.broadcasted_iota(jnp.int32, (8, tb), 0)
        for hh in range(PEER_HEADS):
            grp = s_ref[2 * hh, pl.ds(grp0, 8), :]
            for r in range(n_r):
                row = jnp.sum(jnp.where(sub_iota == i1_0 % 8 + r, grp, 0.0), axis=0, keepdims=True)
                s1b_ref[hh, r] = jnp.broadcast_to(row, (8, tb))

        def d_chunk(r, tc, sub):
            keys = slice(sub * D_SUB, (sub + 1) * D_SUB)
            rows = slice(r * 128 + sub * D_SUB, r * 128 + (sub + 1) * D_SUB)
            cols = slice(tc * 128, (tc + 1) * 128)
            w = jnp.zeros((D_SUB // 8, 8, 128), F32)
            for hh in range(PEER_HEADS):
                s2 = s_ref[2 * hh + 1, keys, cols].reshape(D_SUB // 8, 8, 128)
                x = s2 + s1b_ref[hh, r, :, cols][None]
                w = w + jnp.where(x >= thr_ref[hh, :, cols][None], jnp.exp2(x), 0.0)
            a = act_in[rows, cols]
            gl = 0.5 * a * (1.0 + lax.erf(a * (2.0 ** -0.5)))
            d_out[rows, cols] = (gl * w.reshape(D_SUB, 128)).astype(BF16)

        chunks = [(r, tc, sub) for r in range(n_r) for tc in range(n_tc) for sub in range(128 // D_SUB)]
        per_tick = len(chunks) // (n_q * (tb // MXU_N))
        ci = 0
        for c in range(tb // MXU_N):
            cb = slice(c * MXU_N, (c + 1) * MXU_N)
            acc = None
            for kq in range(n_q):
                kb = slice(kq * k_q, (kq + 1) * k_q)
                part = jnp.dot(u_ref[:, kb], hnt_ref[kb, cb], preferred_element_type=F32)
                acc = part if acc is None else acc + part
                for _ in range(per_tick // 2):
                    d_chunk(*chunks[ci]); ci += 1
                rb = slice(kq * Y_ROWS, (kq + 1) * Y_ROWS)
                yt_ref[rb, cb] += jnp.dot(vt_ref[rb, :], d_in[:, cb], preferred_element_type=F32)
                for _ in range(per_tick - per_tick // 2):
                    d_chunk(*chunks[ci]); ci += 1
            act_out[:, cb] = acc
        assert ci == len(chunks)

    @pl.when(jnp.logical_and(e > 0, e % 2 == 1))
    def _odd():
        body(act0_ref, act1_ref, d1_ref, d0_ref)

    @pl.when(jnp.logical_and(e > 0, e % 2 == 0))
    def _even():
        body(act1_ref, act0_ref, d0_ref, d1_ref)

    @pl.when(e == n_e + 1)
    def _finish():
        y = h_ref[...] + yt_ref[...].T
        if final_norm:
            y = _rms(y, fg_ref[...])
        out_ref[...] = y


def _peer(h, g, wqt_bf, sk_bf, u_bf, vt_bf, fg, final_norm):
    tp, d = h.shape
    ne = u_bf.shape[0]
    tb, nt = PEER_TB, PEER_NT
    n_e = ne // nt
    assert n_e % 2 == 0
    return pl.pallas_call(
        functools.partial(_peer_kernel, final_norm=final_norm, n_e=n_e),
        grid=(tp // tb, n_e + 2),
        in_specs=[pl.BlockSpec((tb, d), lambda t, e: (t, 0)),
                  pl.BlockSpec((1, d), lambda t, e: (0, 0)),
                  pl.BlockSpec(wqt_bf.shape, lambda t, e: (0, 0), pipeline_mode=pl.Buffered(1)),
                  pl.BlockSpec(sk_bf.shape, lambda t, e: (0, 0, 0)),
                  pl.BlockSpec((nt, d), lambda t, e: (jnp.minimum(e, n_e - 1), 0)),
                  pl.BlockSpec((d, nt), lambda t, e: (0, jnp.clip(e - 2, 0, n_e - 1))),
                  pl.BlockSpec((1, d), lambda t, e: (0, 0))],
        out_specs=pl.BlockSpec((tb, d), lambda t, e: (t, 0)),
        out_shape=jax.ShapeDtypeStruct((tp, d), F32),
        scratch_shapes=[pltpu.VMEM((d, tb), BF16),
                        pltpu.VMEM((wqt_bf.shape[0], tb), BF16),
                        pltpu.VMEM((2 * PEER_HEADS, N_KEYS, tb), F32),
                        pltpu.VMEM((2 * PEER_HEADS, PEER_TOPK, tb), F32),
                        pltpu.VMEM((PEER_HEADS, 8, tb), F32),
                        pltpu.VMEM((PEER_HEADS, nt // 128, 8, tb), F32),
                        pltpu.VMEM((nt, tb), F32),
                        pltpu.VMEM((nt, tb), F32),
                        pltpu.VMEM((nt, tb), BF16),
                        pltpu.VMEM((nt, tb), BF16),
                        pltpu.VMEM((d, tb), F32)],
        compiler_params=pltpu.CompilerParams(dimension_semantics=("arbitrary", "arbitrary"),
                                             vmem_limit_bytes=VMEM_LIMIT),
    )(h, g, wqt_bf, sk_bf, u_bf, vt_bf, fg)


def _schedule(seq_lens):
    jpos, nbs, s0, fvr = [], [], [], []
    start = 0
    for s in seq_lens:
        nb = s // BLK + 1
        for j in range(nb):
            jpos.append(j); nbs.append(nb); s0.append(start); fvr.append(META_ROW0 if j == 0 else 0)
        start += nb
    per = max(PEER_TB, IN_TM) // BLK
    while start % per:
        jpos.append(0); nbs.append(1); s0.append(start); fvr.append(BLK)
        start += 1
    return tuple(jnp.asarray(np.asarray(a, np.int32)) for a in (jpos, nbs, s0, fvr)), start


def kernel(x_prompt, x_sample, meta_tokens, ln1_g, w_in, conv_a_w, attn_sink, conf_dw_w, conf_dw_b, conf_ln_g,
           conf_ln_b, g_out_a, g_out_b, g_out_c, w_out, ln2_g, peer_wq, peer_subkeys, peer_u, peer_v, final_g):
    d = x_prompt.shape[-1]
    depth = w_in.shape[0]
    seqs = [x_prompt[b] for b in range(x_prompt.shape[0])] + [x_sample[b] for b in range(x_sample.shape[0])]
    seq_lens = [s.shape[0] for s in seqs]
    assert all(s % BLK == 0 for s in seq_lens)
    sched, nblk = _schedule(seq_lens)
    lead = jnp.concatenate([jnp.zeros((META_ROW0, d), F32), meta_tokens.astype(F32)], axis=0)
    parts = []
    for s in seqs:
        parts += [lead, s]
    n_rows = sum(seq_lens) + BLK * len(seqs)
    if nblk * BLK > n_rows:
        parts.append(jnp.zeros((nblk * BLK - n_rows, d), F32))
    h = jnp.concatenate(parts, axis=0)

    row = lambda a: a.reshape(1, -1)
    for l in range(depth):
        ua, bg, q, kv, glu = _inproj(h, row(ln1_g[l]), w_in[l].astype(BF16))
        h = _mixer(sched, ua, bg, q, kv, glu, h, conv_a_w[l], attn_sink[l], conf_dw_w[l], row(conf_dw_b[l]),
                   row(conf_ln_g[l]), row(conf_ln_b[l]), row(g_out_a[l]), row(g_out_b[l]), row(g_out_c[l]),
                   w_out[l].astype(BF16))
        sk = peer_subkeys[l].reshape(2 * PEER_HEADS, N_KEYS, -1).astype(BF16)
        h = _peer(h, row(ln2_g[l]), peer_wq[l].T.astype(BF16), sk, peer_u[l].astype(BF16),
                  peer_v[l].T.astype(BF16), row(final_g), final_norm=(l == depth - 1))

    outs = []
    off = 0
    for s in seq_lens:
        outs.append(h[off + BLK:off + BLK + s])
        off += BLK + s
    nbp = x_prompt.shape[0]
    return (jnp.stack(outs[:nbp], axis=0), jnp.stack(outs[nbp:], axis=0))
```

```python
import functools

import numpy as np
import jax
import jax.numpy as jnp
from jax import lax
from jax.experimental import pallas as pl
from jax.experimental.pallas import tpu as pltpu

F32 = jnp.float32
BF16 = jnp.bfloat16

N_META = 16
D_SCONV = 512
N_Q_HEADS = 16
N_KV_HEADS = 4
HEAD_DIM = 64
GQA_GROUP = N_Q_HEADS // N_KV_HEADS
D_ATTN = N_Q_HEADS * HEAD_DIM
D_KV = N_KV_HEADS * HEAD_DIM
D_CONF = 512
SHORT_CONV_W = 3
CONF_CONV_W = 31
WINDOW = 128
BLK = 128
META_ROW0 = BLK - N_META
N_KEYS = 128
PEER_HEADS = 8
PEER_TOPK = 16
RMS_EPS = 1e-6
LN_EPS = 1e-5
NEG = -1e30
NEG_INF = float("-inf")
POS_INF = float("inf")
LOG2E = 1.4426950408889634

IN_TM = 256
PEER_TB = 512
PEER_NT = 512
MXU_N = 256
Y_ROWS = 512
ROUTE_ILP = 2
D_SUB = 64
VMEM_LIMIT = 56 * 1024 * 1024


def _rms(x, g):
    return x * lax.rsqrt(jnp.mean(x * x, axis=-1, keepdims=True) + RMS_EPS) * g


def _inproj_kernel(h_ref, g_ref, w_ref, ua_ref, bg_ref, q_ref, kv_ref, glu_ref):
    xn = _rms(h_ref[...], g_ref[...]).astype(BF16)
    z = jnp.dot(xn, w_ref[...], preferred_element_type=F32)
    o = 0
    a_x = z[:, o:o + D_SCONV]; o += D_SCONV
    a_bg = z[:, o:o + D_SCONV]; o += D_SCONV
    a_cg = z[:, o:o + D_SCONV]; o += D_SCONV
    q = z[:, o:o + D_ATTN]; o += D_ATTN
    kv = z[:, o:o + 2 * D_KV]; o += 2 * D_KV
    c_a = z[:, o:o + D_CONF]; o += D_CONF
    c_g = z[:, o:o + D_CONF]
    ua_ref[...] = a_cg * a_x
    bg_ref[...] = a_bg
    q_ref[...] = (q * (HEAD_DIM ** -0.5 * LOG2E)).astype(BF16)
    kv_ref[...] = kv.astype(BF16)
    glu_ref[...] = c_a * jax.nn.sigmoid(c_g)


def _inproj(h, g, w_bf):
    tp, d = h.shape
    n = w_bf.shape[1]
    row = lambda w: pl.BlockSpec((IN_TM, w), lambda i: (i, 0))
    return pl.pallas_call(
        _inproj_kernel,
        grid=(tp // IN_TM,),
        in_specs=[row(d),
                  pl.BlockSpec((1, d), lambda i: (0, 0)),
                  pl.BlockSpec((d, n), lambda i: (0, 0), pipeline_mode=pl.Buffered(1))],
        out_specs=[row(D_SCONV), row(D_SCONV), row(D_ATTN), row(2 * D_KV), row(D_CONF)],
        out_shape=[jax.ShapeDtypeStruct((tp, D_SCONV), F32),
                   jax.ShapeDtypeStruct((tp, D_SCONV), F32),
                   jax.ShapeDtypeStruct((tp, D_ATTN), BF16),
                   jax.ShapeDtypeStruct((tp, 2 * D_KV), BF16),
                   jax.ShapeDtypeStruct((tp, D_CONF), F32)],
        compiler_params=pltpu.CompilerParams(dimension_semantics=("arbitrary",),
                                             vmem_limit_bytes=VMEM_LIMIT),
    )(h, g, w_bf)


def _mixer_kernel(jpos_ref, nbs_ref, s0_ref, fvr_ref,
                  ua_p, ua_c, ua_n, bg_ref, q_ref, kv_p, kv_c, kv_n, kv_m, glu_p, glu_c, glu_n, h_ref,
                  cw_ref, sink_ref, dw_ref, db_ref, lg_ref, lb_ref, ga_ref, gb_ref, gc_ref, wo_ref,
                  out_ref, xa_ref, xc_ref, cc_ref, yb_ref):
    i = pl.program_id(0)
    nblk = pl.num_programs(0)
    j = jpos_ref[i]
    nb = nbs_ref[i]
    has_prev = (i > 0).astype(F32)
    has_next = (i < nblk - 1).astype(F32)

    xa_ref[0:8, :] = ua_p[...] * has_prev
    xa_ref[8:8 + BLK, :] = ua_c[...]
    xa_ref[8 + BLK:16 + BLK, :] = ua_n[...] * has_next
    conv_a = jnp.zeros((BLK, D_SCONV), F32)
    for k in range(SHORT_CONV_W):
        conv_a = conv_a + cw_ref[k:k + 1, :] * xa_ref[pl.ds(8 + k - SHORT_CONV_W // 2, BLK), :]
    y_a = _rms(bg_ref[...] * conv_a, ga_ref[...])

    n_grp = BLK // 8
    n_slab = D_CONF // 128
    for cs in range(n_slab):
        cols = slice(cs * 128, (cs + 1) * 128)
        xc_ref[cs, 0:16, :] = glu_p[:, cols] * has_prev
        xc_ref[cs, 16:16 + BLK, :] = glu_c[:, cols]
        xc_ref[cs, 16 + BLK:32 + BLK, :] = glu_n[:, cols] * has_next
        accs = [jnp.zeros((8, 128), F32) for _ in range(n_grp)]
        for k in range(CONF_CONV_W):
            wk = jnp.broadcast_to(dw_ref[k:k + 1, cols], (8, 128))
            for g in range(n_grp):
                accs[g] = accs[g] + wk * xc_ref[cs, pl.ds(g + k + 16 - CONF_CONV_W // 2, 8, stride=n_grp), :]
        for g in range(n_grp):
            cc_ref[cs, g * 8:(g + 1) * 8, :] = accs[g] + db_ref[:, cols]
    conv_c = jnp.concatenate([cc_ref[cs] for cs in range(n_slab)], axis=1)
    mu = jnp.mean(conv_c, axis=-1, keepdims=True)
    xc = conv_c - mu
    var = jnp.mean(xc * xc, axis=-1, keepdims=True)
    ln = xc * lax.rsqrt(var + LN_EPS) * lg_ref[...] + lb_ref[...]
    y_c_strided = _rms(ln * jax.nn.sigmoid(ln), gc_ref[...])
    for cs in range(n_slab):
        cc_ref[cs] = y_c_strided[:, cs * 128:(cs + 1) * 128]
    y_c = jnp.concatenate(
        [jnp.concatenate([cc_ref[cs, pl.ds((q % 2) * (BLK // 2) + q // 2, 8, stride=8), :]
                          for cs in range(n_slab)], axis=1) for q in range(n_grp)], axis=0)

    rq = lax.broadcasted_iota(jnp.int32, (BLK, BLK), 0)
    rk = lax.broadcasted_iota(jnp.int32, (BLK, BLK), 1)
    jb = jnp.full((BLK, BLK), j, jnp.int32)
    ok_p = jnp.logical_and(rk >= rq, jb >= 2)
    ok_c = jb >= 1
    ok_n = jnp.logical_and(rk <= rq, jb <= nb - 2)
    d_p = (rq + BLK - rk).astype(F32)
    d_c = jnp.abs(rq - rk).astype(F32)
    d_n = (BLK + rk - rq).astype(F32)
    qpos = j * BLK - META_ROW0 + lax.broadcasted_iota(jnp.int32, (BLK, N_META), 0)
    mpos = lax.broadcasted_iota(jnp.int32, (BLK, N_META), 1)
    d_m = jnp.minimum(jnp.abs(qpos - mpos), WINDOW).astype(F32)
    nt = (((1,), (1,)), ((), ()))
    lane = lax.broadcasted_iota(jnp.int32, (1, 2 * HEAD_DIM), 1)

    def value_tile(ref, g):
        slab = ref[:, D_KV + (g // 2) * 2 * HEAD_DIM:D_KV + (g // 2 + 1) * 2 * HEAD_DIM]
        own = (lane < HEAD_DIM) if g % 2 == 0 else (lane >= HEAD_DIM)
        ones_at = HEAD_DIM if g % 2 == 0 else 0
        return jnp.where(own, slab, (lane == ones_at).astype(BF16))

    for g in range(N_KV_HEADS):
        ks = slice(g * HEAD_DIM, (g + 1) * HEAD_DIM)
        k_p, k_c, k_n, k_m = kv_p[:, ks], kv_c[:, ks], kv_n[:, ks], kv_m[:, ks]
        v_p, v_c, v_n, v_m = value_tile(kv_p, g), value_tile(kv_c, g), value_tile(kv_n, g), value_tile(kv_m, g)
        o_lo = 0 if g % 2 == 0 else HEAD_DIM
        sum_at = HEAD_DIM if g % 2 == 0 else 0
        for hd in range(g * GQA_GROUP, (g + 1) * GQA_GROUP):
            slope = float(2.0 ** (-8.0 * (hd + 1) / N_Q_HEADS)) * LOG2E
            qh = q_ref[:, hd * HEAD_DIM:(hd + 1) * HEAD_DIM]
            s_p = lax.dot_general(qh, k_p, nt, preferred_element_type=F32)
            s_c = lax.dot_general(qh, k_c, nt, preferred_element_type=F32)
            s_n = lax.dot_general(qh, k_n, nt, preferred_element_type=F32)
            s_m = lax.dot_general(qh, k_m, nt, preferred_element_type=F32)
            s_p = jnp.where(ok_p, s_p - slope * d_p, NEG)
            s_c = jnp.where(ok_c, s_c - slope * d_c, NEG)
            s_n = jnp.where(ok_n, s_n - slope * d_n, NEG)
            s_m = s_m - slope * d_m
            sink = sink_ref[hd] * LOG2E
            mx = jnp.maximum(jnp.max(jnp.maximum(jnp.maximum(s_p, s_c), s_n), axis=-1, keepdims=True),
                             jnp.max(s_m, axis=-1, keepdims=True))
            mx = jnp.maximum(mx, sink)
            o = (jnp.dot(jnp.exp2(s_p - mx).astype(BF16), v_p, preferred_element_type=F32)
                 + jnp.dot(jnp.exp2(s_c - mx).astype(BF16), v_c, preferred_element_type=F32)
                 + jnp.dot(jnp.exp2(s_n - mx).astype(BF16), v_n, preferred_element_type=F32)
                 + jnp.dot(jnp.exp2(s_m - mx).astype(BF16), v_m, preferred_element_type=F32))
            den = o[:, sum_at:sum_at + 1] + jnp.exp2(sink - mx)
            yb_ref[:, hd * HEAD_DIM:(hd + 1) * HEAD_DIM] = o[:, o_lo:o_lo + HEAD_DIM] / den
    y_b = _rms(yb_ref[...], gb_ref[...])

    out = (jnp.dot(y_a.astype(BF16), wo_ref[0:D_SCONV, :], preferred_element_type=F32)
           + jnp.dot(y_b.astype(BF16), wo_ref[D_SCONV:D_SCONV + D_ATTN, :], preferred_element_type=F32)
           + jnp.dot(y_c.astype(BF16), wo_ref[D_SCONV + D_ATTN:, :], preferred_element_type=F32))
    keep = lax.broadcasted_iota(jnp.int32, (BLK, 1), 0) >= fvr_ref[i]
    out_ref[...] = jnp.where(keep, h_ref[...] + out, 0.0)


def _mixer(sched, ua, bg, q, kv, glu, h, cw, sink, dw, db, lg, lb, ga, gb, gc, wo_bf):
    tp, d = h.shape
    nblk = tp // BLK
    jpos, nbs, s0, fvr = sched
    cur = lambda w: pl.BlockSpec((BLK, w), lambda i, *_: (i, 0))
    prv = lambda w: pl.BlockSpec((BLK, w), lambda i, *_: (jnp.maximum(i - 1, 0), 0))
    nxt = lambda w: pl.BlockSpec((BLK, w), lambda i, *_: (jnp.minimum(i + 1, nblk - 1), 0))
    halo_p = lambda r, w: pl.BlockSpec((r, w), lambda i, *_: (jnp.maximum(i * (BLK // r) - 1, 0), 0))
    halo_n = lambda r, w: pl.BlockSpec((r, w), lambda i, *_: (jnp.minimum((i + 1) * (BLK // r), nblk * (BLK // r) - 1), 0))
    meta = pl.BlockSpec((N_META, 2 * D_KV), lambda i, jp, nb, s0r, fv: (s0r[i] * (BLK // N_META) + BLK // N_META - 1, 0))
    full = lambda a: pl.BlockSpec(a.shape, lambda i, *_: (0,) * a.ndim)
    grid_spec = pltpu.PrefetchScalarGridSpec(
        num_scalar_prefetch=4,
        grid=(nblk,),
        in_specs=[halo_p(8, D_SCONV), cur(D_SCONV), halo_n(8, D_SCONV), cur(D_SCONV), cur(D_ATTN),
                  prv(2 * D_KV), cur(2 * D_KV), nxt(2 * D_KV), meta,
                  halo_p(16, D_CONF), cur(D_CONF), halo_n(16, D_CONF), cur(d),
                  full(cw), pl.BlockSpec(memory_space=pltpu.SMEM), full(dw), full(db), full(lg), full(lb),
                  full(ga), full(gb), full(gc),
                  pl.BlockSpec(wo_bf.shape, lambda i, *_: (0, 0), pipeline_mode=pl.Buffered(1))],
        out_specs=cur(d),
        scratch_shapes=[pltpu.VMEM((BLK + 16, D_SCONV), F32),
                        pltpu.VMEM((D_CONF // 128, BLK + 32, 128), F32),
                        pltpu.VMEM((D_CONF // 128, BLK, 128), F32),
                        pltpu.VMEM((BLK, D_ATTN), F32)],
    )
    return pl.pallas_call(
        _mixer_kernel,
        grid_spec=grid_spec,
        out_shape=jax.ShapeDtypeStruct((tp, d), F32),
        compiler_params=pltpu.CompilerParams(dimension_semantics=("arbitrary",),
                                             vmem_limit_bytes=VMEM_LIMIT),
    )(jpos, nbs, s0, fvr, ua, ua, ua, bg, q, kv, kv, kv, kv, glu, glu, glu, h,
      cw, sink, dw, db, lg, lb, ga, gb, gc, wo_bf)


def _peer_route(h_ref, g_ref, wqt_ref, sk_ref, hnt_ref, qt_ref, s_ref, top_ref, thr_ref):
    tb = h_ref.shape[0]
    n_tc = tb // 128
    xn = _rms(h_ref[...], g_ref[...])
    hnt_ref[...] = xn.T.astype(BF16)
    qt_ref[...] = jnp.dot(wqt_ref[...], hnt_ref[...], preferred_element_type=F32).astype(BF16)

    def score_body(hc, carry):
        row0 = pl.multiple_of(hc * 128, 128)
        s_ref[hc] = jnp.dot(sk_ref[hc], qt_ref[pl.ds(row0, 128), :], preferred_element_type=F32) * LOG2E
        return carry
    lax.fori_loop(0, 2 * PEER_HEADS, score_body, 0)

    n_grp = n_tc // ROUTE_ILP

    def chunk_cols(it):
        return [pl.ds(pl.multiple_of(((it % n_grp) * ROUTE_ILP + u) * 128, 128), 128) for u in range(ROUTE_ILP)]

    def top_body(it, carry):
        hc = it // n_grp
        cols = chunk_cols(it)
        cur = [s_ref[hc, :, c] for c in cols]
        for k in range(PEER_TOPK):
            m = [jnp.max(x, axis=0, keepdims=True) for x in cur]
            for c, mm in zip(cols, m):
                top_ref[hc, k:k + 1, c] = mm
            cur = [jnp.where(x == mm, NEG_INF, x) for x, mm in zip(cur, m)]
        return carry
    lax.fori_loop(0, 2 * PEER_HEADS * n_grp, top_body, 0)

    row8 = lax.broadcasted_iota(jnp.int32, (8, 128), 0)
    row16 = lax.broadcasted_iota(jnp.int32, (PEER_TOPK, 128), 0)
    n_lead = 4

    def pair_sums(a_rows, a_all, b_all, b_rows):
        out = [a_rows[0] + b_all]
        for i in range(1, n_lead):
            out.append(jnp.where(row8 < PEER_TOPK // (i + 1), a_rows[i] + b_all[0:8], NEG_INF))
        out.append(jnp.where(row16 >= n_lead, a_all + b_rows[0], NEG_INF))
        for j in range(1, PEER_TOPK // (n_lead + 1)):
            out.append(jnp.where(jnp.logical_and(row8 >= n_lead, row8 < PEER_TOPK // (j + 1)),
                                 a_all[0:8] + b_rows[j], NEG_INF))
        return out

    def fold8(x, op):
        return x if x.shape[0] == 8 else op(x[0:8], x[8:16])

    def reduce_all(xs, op, red):
        acc = fold8(xs[0], op)
        for x in xs[1:]:
            acc = op(acc, fold8(x, op))
        return red(acc, axis=0, keepdims=True)

    def cand_body(it, carry):
        hh = it // n_grp
        cols = chunk_cols(it)
        n_b = PEER_TOPK // (n_lead + 1)
        b_all = [top_ref[2 * hh + 1, :, c] for c in cols]
        b_rows = [[top_ref[2 * hh + 1, j:j + 1, c] for j in range(n_b)] for c in cols]
        m = [top_ref[2 * hh, 0:1, c] + br[0] for c, br in zip(cols, b_rows)]
        a_all = [top_ref[2 * hh, :, c] - mm for c, mm in zip(cols, m)]
        a_rows = [[top_ref[2 * hh, i:i + 1, c] - mm for i in range(n_lead)] for c, mm in zip(cols, m)]
        cands = [pair_sums(ar, aa, ba, br) for ar, aa, ba, br in zip(a_rows, a_all, b_all, b_rows)]
        cur = [list(cs) for cs in cands]
        thr = None
        for k in range(PEER_TOPK):
            mx = [reduce_all(cs, jnp.maximum, jnp.max) for cs in cur]
            if k == PEER_TOPK - 1:
                thr = mx
            else:
                cur = [[jnp.where(x == mm, NEG_INF, x) for x in cs] for cs, mm in zip(cur, mx)]
        for u, c in enumerate(cols):
            z = reduce_all([jnp.where(x >= thr[u], jnp.exp2(x), 0.0) for x in cands[u]], jnp.add, jnp.sum)
            logz = jnp.log2(z)
            shifted = pair_sums([r - logz for r in a_rows[u]], a_all[u] - logz, b_all[u], b_rows[u])
            tz = reduce_all([jnp.where(x >= thr[u], y, POS_INF) for x, y in zip(cands[u], shifted)],
                            jnp.minimum, jnp.min)
            thr_ref[hh, :, c] = jnp.broadcast_to(tz, (8, 128))
            s_ref[2 * hh, :, c] = (s_ref[2 * hh, :, c] - m[u]) - logz
        return carry
    lax.fori_loop(0, PEER_HEADS * n_grp, cand_body, 0)


def _peer_kernel(h_ref, g_ref, wqt_ref, sk_ref, u_ref, vt_ref, fg_ref, out_ref,
                 hnt_ref, qt_ref, s_ref, top_ref, thr_ref, s1b_ref, act0_ref, act1_ref, d0_ref, d1_ref, yt_ref,
                 *, final_norm, n_e):
    e = pl.program_id(1)
    tb = h_ref.shape[0]
    n_tc = tb // 128
    nt = u_ref.shape[0]
    n_r = nt // 128

    @pl.when(e == 0)
    def _prologue():
        _peer_route(h_ref, g_ref, wqt_ref, sk_ref, hnt_ref, qt_ref, s_ref, top_ref, thr_ref)
        act0_ref[...] = jnp.dot(u_ref[...], hnt_ref[...], preferred_element_type=F32)
        d1_ref[...] = jnp.zeros_like(d1_ref)
        yt_ref[...] = jnp.zeros_like(yt_ref)

    def body(act_in, act_out, d_in, d_out):
        d_model = vt_ref.shape[0]
        n_q = d_model // Y_ROWS
        k_q = d_model // n_q

        i1_0 = jnp.minimum((e - 1) * n_r, N_KEYS - n_r)
        grp0 = pl.multiple_of((i1_0 // 8) * 8, 8)
        sub_iota = lax.broadcasted_iota(jnp.int32, (8, tb), 0)
        for hh in range(PEER_HEADS):
            grp = s_ref[2 * hh, pl.ds(grp0, 8), :]
            for r in range(n_r):
                row = jnp.sum(jnp.where(sub_iota == i1_0 % 8 + r, grp, 0.0), axis=0, keepdims=True)
                s1b_ref[hh, r] = jnp.broadcast_to(row, (8, tb))

        def d_chunk(r, tc, sub):
            keys = slice(sub * D_SUB, (sub + 1) * D_SUB)
            rows = slice(r * 128 + sub * D_SUB, r * 128 + (sub + 1) * D_SUB)
            cols = slice(tc * 128, (tc + 1) * 128)
            w = jnp.zeros((D_SUB // 8, 8, 128), F32)
            for hh in range(PEER_HEADS):
                s2 = s_ref[2 * hh + 1, keys, cols].reshape(D_SUB // 8, 8, 128)
                x = s2 + s1b_ref[hh, r, :, cols][None]
                w = w + jnp.where(x >= thr_ref[hh, :, cols][None], jnp.exp2(x), 0.0)
            a = act_in[rows, cols]
            gl = 0.5 * a * (1.0 + lax.erf(a * (2.0 ** -0.5)))
            d_out[rows, cols] = (gl * w.reshape(D_SUB, 128)).astype(BF16)

        chunks = [(r, tc, sub) for r in range(n_r) for tc in range(n_tc) for sub in range(128 // D_SUB)]
        per_tick = len(chunks) // (n_q * (tb // MXU_N))
        ci = 0
        for c in range(tb // MXU_N):
            cb = slice(c * MXU_N, (c + 1) * MXU_N)
            acc = None
            for kq in range(n_q):
                kb = slice(kq * k_q, (kq + 1) * k_q)
                part = jnp.dot(u_ref[:, kb], hnt_ref[kb, cb], preferred_element_type=F32)
                acc = part if acc is None else acc + part
                for _ in range(per_tick // 2):
                    d_chunk(*chunks[ci]); ci += 1
                rb = slice(kq * Y_ROWS, (kq + 1) * Y_ROWS)
                yt_ref[rb, cb] += jnp.dot(vt_ref[rb, :], d_in[:, cb], preferred_element_type=F32)
                for _ in range(per_tick - per_tick // 2):
                    d_chunk(*chunks[ci]); ci += 1
            act_out[:, cb] = acc
        assert ci == len(chunks)

    @pl.when(jnp.logical_and(e > 0, e % 2 == 1))
    def _odd():
        body(act0_ref, act1_ref, d1_ref, d0_ref)

    @pl.when(jnp.logical_and(e > 0, e % 2 == 0))
    def _even():
        body(act1_ref, act0_ref, d0_ref, d1_ref)

    @pl.when(e == n_e + 1)
    def _finish():
        y = h_ref[...] + yt_ref[...].T
        if final_norm:
            y = _rms(y, fg_ref[...])
        out_ref[...] = y


def _peer(h, g, wqt_bf, sk_bf, u_bf, vt_bf, fg, final_norm):
    tp, d = h.shape
    ne = u_bf.shape[0]
    tb, nt = PEER_TB, PEER_NT
    n_e = ne // nt
    assert n_e % 2 == 0
    return pl.pallas_call(
        functools.partial(_peer_kernel, final_norm=final_norm, n_e=n_e),
        grid=(tp // tb, n_e + 2),
        in_specs=[pl.BlockSpec((tb, d), lambda t, e: (t, 0)),
                  pl.BlockSpec((1, d), lambda t, e: (0, 0)),
                  pl.BlockSpec(wqt_bf.shape, lambda t, e: (0, 0), pipeline_mode=pl.Buffered(1)),
                  pl.BlockSpec(sk_bf.shape, lambda t, e: (0, 0, 0)),
                  pl.BlockSpec((nt, d), lambda t, e: (jnp.minimum(e, n_e - 1), 0)),
                  pl.BlockSpec((d, nt), lambda t, e: (0, jnp.clip(e - 2, 0, n_e - 1))),
                  pl.BlockSpec((1, d), lambda t, e: (0, 0))],
        out_specs=pl.BlockSpec((tb, d), lambda t, e: (t, 0)),
        out_shape=jax.ShapeDtypeStruct((tp, d), F32),
        scratch_shapes=[pltpu.VMEM((d, tb), BF16),
                        pltpu.VMEM((wqt_bf.shape[0], tb), BF16),
                        pltpu.VMEM((2 * PEER_HEADS, N_KEYS, tb), F32),
                        pltpu.VMEM((2 * PEER_HEADS, PEER_TOPK, tb), F32),
                        pltpu.VMEM((PEER_HEADS, 8, tb), F32),
                        pltpu.VMEM((PEER_HEADS, nt // 128, 8, tb), F32),
                        pltpu.VMEM((nt, tb), F32),
                        pltpu.VMEM((nt, tb), F32),
                        pltpu.VMEM((nt, tb), BF16),
                        pltpu.VMEM((nt, tb), BF16),
                        pltpu.VMEM((d, tb), F32)],
        compiler_params=pltpu.CompilerParams(dimension_semantics=("arbitrary", "arbitrary"),
                                             vmem_limit_bytes=VMEM_LIMIT),
    )(h, g, wqt_bf, sk_bf, u_bf, vt_bf, fg)


def _schedule(seq_lens):
    jpos, nbs, s0, fvr = [], [], [], []
    start = 0
    for s in seq_lens:
        nb = s // BLK + 1
        for j in range(nb):
            jpos.append(j); nbs.append(nb); s0.append(start); fvr.append(META_ROW0 if j == 0 else 0)
        start += nb
    per = max(PEER_TB, IN_TM) // BLK
    while start % per:
        jpos.append(0); nbs.append(1); s0.append(start); fvr.append(BLK)
        start += 1
    return tuple(jnp.asarray(np.asarray(a, np.int32)) for a in (jpos, nbs, s0, fvr)), start


def kernel(x_prompt, x_sample, meta_tokens, ln1_g, w_in, conv_a_w, attn_sink, conf_dw_w, conf_dw_b, conf_ln_g,
           conf_ln_b, g_out_a, g_out_b, g_out_c, w_out, ln2_g, peer_wq, peer_subkeys, peer_u, peer_v, final_g):
    d = x_prompt.shape[-1]
    depth = w_in.shape[0]
    seqs = [x_prompt[b] for b in range(x_prompt.shape[0])] + [x_sample[b] for b in range(x_sample.shape[0])]
    seq_lens = [s.shape[0] for s in seqs]
    assert all(s % BLK == 0 for s in seq_lens)
    sched, nblk = _schedule(seq_lens)
    lead = jnp.concatenate([jnp.zeros((META_ROW0, d), F32), meta_tokens.astype(F32)], axis=0)
    parts = []
    for s in seqs:
        parts += [lead, s]
    n_rows = sum(seq_lens) + BLK * len(seqs)
    if nblk * BLK > n_rows:
        parts.append(jnp.zeros((nblk * BLK - n_rows, d), F32))
    h = jnp.concatenate(parts, axis=0)

    row = lambda a: a.reshape(1, -1)
    for l in range(depth):
        ua, bg, q, kv, glu = _inproj(h, row(ln1_g[l]), w_in[l].astype(BF16))
        h = _mixer(sched, ua, bg, q, kv, glu, h, conv_a_w[l], attn_sink[l], conf_dw_w[l], row(conf_dw_b[l]),
                   row(conf_ln_g[l]), row(conf_ln_b[l]), row(g_out_a[l]), row(g_out_b[l]), row(g_out_c[l]),
                   w_out[l].astype(BF16))
        sk = peer_subkeys[l].reshape(2 * PEER_HEADS, N_KEYS, -1).astype(BF16)
        h = _peer(h, row(ln2_g[l]), peer_wq[l].T.astype(BF16), sk, peer_u[l].astype(BF16),
                  peer_v[l].T.astype(BF16), row(final_g), final_norm=(l == depth - 1))

    outs = []
    off = 0
    for s in seq_lens:
        outs.append(h[off + BLK:off + BLK + s])
        off += BLK + s
    nbp = x_prompt.shape[0]
    return (jnp.stack(outs[:nbp], axis=0), jnp.stack(outs[nbp:], axis=0))
```

```python
import functools

import numpy as np
import jax
import jax.numpy as jnp
from jax import lax
from jax.experimental import pallas as pl
from jax.experimental.pallas import tpu as pltpu

F32 = jnp.float32
BF16 = jnp.bfloat16

N_META = 16
D_SCONV = 512
N_Q_HEADS = 16
N_KV_HEADS = 4
HEAD_DIM = 64
GQA_GROUP = N_Q_HEADS // N_KV_HEADS
D_ATTN = N_Q_HEADS * HEAD_DIM
D_KV = N_KV_HEADS * HEAD_DIM
D_CONF = 512
SHORT_CONV_W = 3
CONF_CONV_W = 31
WINDOW = 128
BLK = 128
META_ROW0 = BLK - N_META
N_KEYS = 128
PEER_HEADS = 8
PEER_TOPK = 16
RMS_EPS = 1e-6
LN_EPS = 1e-5
NEG = -1e30
NEG_INF = float("-inf")
POS_INF = float("inf")
LOG2E = 1.4426950408889634

IN_TM = 256
PEER_TB = 512
PEER_NT = 512
MXU_N = 256
Y_ROWS = 512
ROUTE_ILP = 2
D_SUB = 64
VMEM_LIMIT = 56 * 1024 * 1024


def _rms(x, g):
    return x * lax.rsqrt(jnp.mean(x * x, axis=-1, keepdims=True) + RMS_EPS) * g


def _inproj_kernel(h_ref, g_ref, w_ref, ua_ref, bg_ref, q_ref, kv_ref, glu_ref):
    xn = _rms(h_ref[...], g_ref[...]).astype(BF16)
    z = jnp.dot(xn, w_ref[...], preferred_element_type=F32)
    o = 0
    a_x = z[:, o:o + D_SCONV]; o += D_SCONV
    a_bg = z[:, o:o + D_SCONV]; o += D_SCONV
    a_cg = z[:, o:o + D_SCONV]; o += D_SCONV
    q = z[:, o:o + D_ATTN]; o += D_ATTN
    kv = z[:, o:o + 2 * D_KV]; o += 2 * D_KV
    c_a = z[:, o:o + D_CONF]; o += D_CONF
    c_g = z[:, o:o + D_CONF]
    ua_ref[...] = a_cg * a_x
    bg_ref[...] = a_bg
    q_ref[...] = (q * (HEAD_DIM ** -0.5 * LOG2E)).astype(BF16)
    kv_ref[...] = kv.astype(BF16)
    glu_ref[...] = c_a * jax.nn.sigmoid(c_g)


def _inproj(h, g, w_bf):
    tp, d = h.shape
    n = w_bf.shape[1]
    row = lambda w: pl.BlockSpec((IN_TM, w), lambda i: (i, 0))
    return pl.pallas_call(
        _inproj_kernel,
        grid=(tp // IN_TM,),
        in_specs=[row(d),
                  pl.BlockSpec((1, d), lambda i: (0, 0)),
                  pl.BlockSpec((d, n), lambda i: (0, 0), pipeline_mode=pl.Buffered(1))],
        out_specs=[row(D_SCONV), row(D_SCONV), row(D_ATTN), row(2 * D_KV), row(D_CONF)],
        out_shape=[jax.ShapeDtypeStruct((tp, D_SCONV), F32),
                   jax.ShapeDtypeStruct((tp, D_SCONV), F32),
                   jax.ShapeDtypeStruct((tp, D_ATTN), BF16),
                   jax.ShapeDtypeStruct((tp, 2 * D_KV), BF16),
                   jax.ShapeDtypeStruct((tp, D_CONF), F32)],
        compiler_params=pltpu.CompilerParams(dimension_semantics=("arbitrary",),
                                             vmem_limit_bytes=VMEM_LIMIT),
    )(h, g, w_bf)


def _mixer_kernel(jpos_ref, nbs_ref, prv_ref, nxt_ref, mta_ref, fvr_ref,
                  ua_p, ua_c, ua_n, bg_ref, q_ref, kv_p, kv_c, kv_n, kv_m, glu_p, glu_c, glu_n, h_ref,
                  cw_ref, sink_ref, dw_ref, db_ref, lg_ref, lb_ref, ga_ref, gb_ref, gc_ref, wo_ref,
                  out_ref, xa_ref, xc_ref, cc_ref, yb_ref):
    i = pl.program_id(0)
    j = jpos_ref[i]
    nb = nbs_ref[i]

    xa_ref[0:8, :] = ua_p[...]
    xa_ref[8:8 + BLK, :] = ua_c[...]
    xa_ref[8 + BLK:16 + BLK, :] = ua_n[...]
    conv_a = jnp.zeros((BLK, D_SCONV), F32)
    for k in range(SHORT_CONV_W):
        conv_a = conv_a + cw_ref[k:k + 1, :] * xa_ref[pl.ds(8 + k - SHORT_CONV_W // 2, BLK), :]
    y_a = _rms(bg_ref[...] * conv_a, ga_ref[...])

    n_grp = BLK // 8
    n_slab = D_CONF // 128
    for cs in range(n_slab):
        cols = slice(cs * 128, (cs + 1) * 128)
        xc_ref[cs, 0:16, :] = glu_p[:, cols]
        xc_ref[cs, 16:16 + BLK, :] = glu_c[:, cols]
        xc_ref[cs, 16 + BLK:32 + BLK, :] = glu_n[:, cols]
        accs = [jnp.zeros((8, 128), F32) for _ in range(n_grp)]
        for k in range(CONF_CONV_W):
            wk = jnp.broadcast_to(dw_ref[k:k + 1, cols], (8, 128))
            for g in range(n_grp):
                accs[g] = accs[g] + wk * xc_ref[cs, pl.ds(g + k + 16 - CONF_CONV_W // 2, 8, stride=n_grp), :]
        for g in range(n_grp):
            cc_ref[cs, g * 8:(g + 1) * 8, :] = accs[g] + db_ref[:, cols]
    conv_c = jnp.concatenate([cc_ref[cs] for cs in range(n_slab)], axis=1)
    mu = jnp.mean(conv_c, axis=-1, keepdims=True)
    xc = conv_c - mu
    var = jnp.mean(xc * xc, axis=-1, keepdims=True)
    ln = xc * lax.rsqrt(var + LN_EPS) * lg_ref[...] + lb_ref[...]
    y_c_strided = _rms(ln * jax.nn.sigmoid(ln), gc_ref[...])
    for cs in range(n_slab):
        cc_ref[cs] = y_c_strided[:, cs * 128:(cs + 1) * 128]
    y_c = jnp.concatenate(
        [jnp.concatenate([cc_ref[cs, pl.ds((q % 2) * (BLK // 2) + q // 2, 8, stride=8), :]
                          for cs in range(n_slab)], axis=1) for q in range(n_grp)], axis=0)

    rq = lax.broadcasted_iota(jnp.int32, (BLK, BLK), 0)
    rk = lax.broadcasted_iota(jnp.int32, (BLK, BLK), 1)
    jb = jnp.full((BLK, BLK), j, jnp.int32)
    ok_p = jnp.logical_and(rk >= rq, jb >= 2)
    ok_c = jb >= 1
    ok_n = jnp.logical_and(rk <= rq, jb <= nb - 2)
    d_p = (rq + BLK - rk).astype(F32)
    d_c = jnp.abs(rq - rk).astype(F32)
    d_n = (BLK + rk - rq).astype(F32)
    qpos = j * BLK - META_ROW0 + lax.broadcasted_iota(jnp.int32, (BLK, N_META), 0)
    mpos = lax.broadcasted_iota(jnp.int32, (BLK, N_META), 1)
    d_m = jnp.minimum(jnp.abs(qpos - mpos), WINDOW).astype(F32)
    nt = (((1,), (1,)), ((), ()))
    lane = lax.broadcasted_iota(jnp.int32, (1, 2 * HEAD_DIM), 1)

    def value_tile(ref, g):
        slab = ref[:, D_KV + (g // 2) * 2 * HEAD_DIM:D_KV + (g // 2 + 1) * 2 * HEAD_DIM]
        own = (lane < HEAD_DIM) if g % 2 == 0 else (lane >= HEAD_DIM)
        ones_at = HEAD_DIM if g % 2 == 0 else 0
        return jnp.where(own, slab, (lane == ones_at).astype(BF16))

    for g in range(N_KV_HEADS):
        ks = slice(g * HEAD_DIM, (g + 1) * HEAD_DIM)
        k_p, k_c, k_n, k_m = kv_p[:, ks], kv_c[:, ks], kv_n[:, ks], kv_m[:, ks]
        v_p, v_c, v_n, v_m = value_tile(kv_p, g), value_tile(kv_c, g), value_tile(kv_n, g), value_tile(kv_m, g)
        o_lo = 0 if g % 2 == 0 else HEAD_DIM
        sum_at = HEAD_DIM if g % 2 == 0 else 0
        for hd in range(g * GQA_GROUP, (g + 1) * GQA_GROUP):
            slope = float(2.0 ** (-8.0 * (hd + 1) / N_Q_HEADS)) * LOG2E
            qh = q_ref[:, hd * HEAD_DIM:(hd + 1) * HEAD_DIM]
            s_p = lax.dot_general(qh, k_p, nt, preferred_element_type=F32)
            s_c = lax.dot_general(qh, k_c, nt, preferred_element_type=F32)
            s_n = lax.dot_general(qh, k_n, nt, preferred_element_type=F32)
            s_m = lax.dot_general(qh, k_m, nt, preferred_element_type=F32)
            s_p = jnp.where(ok_p, s_p - slope * d_p, NEG)
            s_c = jnp.where(ok_c, s_c - slope * d_c, NEG)
            s_n = jnp.where(ok_n, s_n - slope * d_n, NEG)
            s_m = s_m - slope * d_m
            sink = sink_ref[hd] * LOG2E
            mx = jnp.maximum(jnp.max(jnp.maximum(jnp.maximum(s_p, s_c), s_n), axis=-1, keepdims=True),
                             jnp.max(s_m, axis=-1, keepdims=True))
            mx = jnp.maximum(mx, sink)
            o = (jnp.dot(jnp.exp2(s_p - mx).astype(BF16), v_p, preferred_element_type=F32)
                 + jnp.dot(jnp.exp2(s_c - mx).astype(BF16), v_c, preferred_element_type=F32)
                 + jnp.dot(jnp.exp2(s_n - mx).astype(BF16), v_n, preferred_element_type=F32)
                 + jnp.dot(jnp.exp2(s_m - mx).astype(BF16), v_m, preferred_element_type=F32))
            den = o[:, sum_at:sum_at + 1] + jnp.exp2(sink - mx)
            yb_ref[:, hd * HEAD_DIM:(hd + 1) * HEAD_DIM] = o[:, o_lo:o_lo + HEAD_DIM] / den
    y_b = _rms(yb_ref[...], gb_ref[...])

    out = (jnp.dot(y_a.astype(BF16), wo_ref[0:D_SCONV, :], preferred_element_type=F32)
           + jnp.dot(y_b.astype(BF16), wo_ref[D_SCONV:D_SCONV + D_ATTN, :], preferred_element_type=F32)
           + jnp.dot(y_c.astype(BF16), wo_ref[D_SCONV + D_ATTN:, :], preferred_element_type=F32))
    keep = lax.broadcasted_iota(jnp.int32, (BLK, 1), 0) >= fvr_ref[i]
    out_ref[...] = jnp.where(keep, h_ref[...] + out, 0.0)


def _mixer(sched, ua, bg, q, kv, glu, h, cw, sink, dw, db, lg, lb, ga, gb, gc, wo_bf):
    tp, d = h.shape
    nblk = tp // BLK
    cur = lambda w: pl.BlockSpec((BLK, w), lambda i, *_: (i, 0))
    prv = lambda w: pl.BlockSpec((BLK, w), lambda i, jp, nb, pv, nx, mt, fv: (pv[i], 0))
    nxt = lambda w: pl.BlockSpec((BLK, w), lambda i, jp, nb, pv, nx, mt, fv: (nx[i], 0))
    halo_p = lambda r, w: pl.BlockSpec((r, w), lambda i, jp, nb, pv, nx, mt, fv: ((pv[i] + 1) * (BLK // r) - 1, 0))
    halo_n = lambda r, w: pl.BlockSpec((r, w), lambda i, jp, nb, pv, nx, mt, fv: (nx[i] * (BLK // r), 0))
    meta = pl.BlockSpec((N_META, 2 * D_KV), lambda i, jp, nb, pv, nx, mt, fv: ((mt[i] + 1) * (BLK // N_META) - 1, 0))
    full = lambda a: pl.BlockSpec(a.shape, lambda i, *_: (0,) * a.ndim)
    grid_spec = pltpu.PrefetchScalarGridSpec(
        num_scalar_prefetch=len(sched),
        grid=(nblk,),
        in_specs=[halo_p(8, D_SCONV), cur(D_SCONV), halo_n(8, D_SCONV), cur(D_SCONV), cur(D_ATTN),
                  prv(2 * D_KV), cur(2 * D_KV), nxt(2 * D_KV), meta,
                  halo_p(16, D_CONF), cur(D_CONF), halo_n(16, D_CONF), cur(d),
                  full(cw), pl.BlockSpec(memory_space=pltpu.SMEM), full(dw), full(db), full(lg), full(lb),
                  full(ga), full(gb), full(gc),
                  pl.BlockSpec(wo_bf.shape, lambda i, *_: (0, 0), pipeline_mode=pl.Buffered(1))],
        out_specs=cur(d),
        scratch_shapes=[pltpu.VMEM((BLK + 16, D_SCONV), F32),
                        pltpu.VMEM((D_CONF // 128, BLK + 32, 128), F32),
                        pltpu.VMEM((D_CONF // 128, BLK, 128), F32),
                        pltpu.VMEM((BLK, D_ATTN), F32)],
    )
    return pl.pallas_call(
        _mixer_kernel,
        grid_spec=grid_spec,
        out_shape=jax.ShapeDtypeStruct((tp, d), F32),
        compiler_params=pltpu.CompilerParams(dimension_semantics=("arbitrary",),
                                             vmem_limit_bytes=VMEM_LIMIT),
    )(*sched, ua, ua, ua, bg, q, kv, kv, kv, kv, glu, glu, glu, h,
      cw, sink, dw, db, lg, lb, ga, gb, gc, wo_bf)


def _peer_route(h_ref, g_ref, wqt_ref, sk_ref, hnt_ref, qt_ref, s_ref, top_ref, thr_ref):
    tb = h_ref.shape[0]
    n_tc = tb // 128
    xn = _rms(h_ref[...], g_ref[...])
    hnt_ref[...] = xn.T.astype(BF16)
    qt_ref[...] = jnp.dot(wqt_ref[...], hnt_ref[...], preferred_element_type=F32).astype(BF16)

    def score_body(hc, carry):
        row0 = pl.multiple_of(hc * 128, 128)
        s_ref[hc] = jnp.dot(sk_ref[hc], qt_ref[pl.ds(row0, 128), :], preferred_element_type=F32) * LOG2E
        return carry
    lax.fori_loop(0, 2 * PEER_HEADS, score_body, 0)

    n_grp = n_tc // ROUTE_ILP

    def chunk_cols(it):
        return [pl.ds(pl.multiple_of(((it % n_grp) * ROUTE_ILP + u) * 128, 128), 128) for u in range(ROUTE_ILP)]

    def top_body(it, carry):
        hc = it // n_grp
        cols = chunk_cols(it)
        cur = [s_ref[hc, :, c] for c in cols]
        for k in range(PEER_TOPK):
            m = [jnp.max(x, axis=0, keepdims=True) for x in cur]
            for c, mm in zip(cols, m):
                top_ref[hc, k:k + 1, c] = mm
            cur = [jnp.where(x == mm, NEG_INF, x) for x, mm in zip(cur, m)]
        return carry
    lax.fori_loop(0, 2 * PEER_HEADS * n_grp, top_body, 0)

    row8 = lax.broadcasted_iota(jnp.int32, (8, 128), 0)
    row16 = lax.broadcasted_iota(jnp.int32, (PEER_TOPK, 128), 0)
    n_lead = 4

    def pair_sums(a_rows, a_all, b_all, b_rows):
        out = [a_rows[0] + b_all]
        for i in range(1, n_lead):
            out.append(jnp.where(row8 < PEER_TOPK // (i + 1), a_rows[i] + b_all[0:8], NEG_INF))
        out.append(jnp.where(row16 >= n_lead, a_all + b_rows[0], NEG_INF))
        for j in range(1, PEER_TOPK // (n_lead + 1)):
            out.append(jnp.where(jnp.logical_and(row8 >= n_lead, row8 < PEER_TOPK // (j + 1)),
                                 a_all[0:8] + b_rows[j], NEG_INF))
        return out

    def fold8(x, op):
        return x if x.shape[0] == 8 else op(x[0:8], x[8:16])

    def reduce_all(xs, op, red):
        acc = fold8(xs[0], op)
        for x in xs[1:]:
            acc = op(acc, fold8(x, op))
        return red(acc, axis=0, keepdims=True)

    def cand_body(it, carry):
        hh = it // n_grp
        cols = chunk_cols(it)
        n_b = PEER_TOPK // (n_lead + 1)
        b_all = [top_ref[2 * hh + 1, :, c] for c in cols]
        b_rows = [[top_ref[2 * hh + 1, j:j + 1, c] for j in range(n_b)] for c in cols]
        m = [top_ref[2 * hh, 0:1, c] + br[0] for c, br in zip(cols, b_rows)]
        a_all = [top_ref[2 * hh, :, c] - mm for c, mm in zip(cols, m)]
        a_rows = [[top_ref[2 * hh, i:i + 1, c] - mm for i in range(n_lead)] for c, mm in zip(cols, m)]
        cands = [pair_sums(ar, aa, ba, br) for ar, aa, ba, br in zip(a_rows, a_all, b_all, b_rows)]
        cur = [list(cs) for cs in cands]
        thr = None
        for k in range(PEER_TOPK):
            mx = [reduce_all(cs, jnp.maximum, jnp.max) for cs in cur]
            if k == PEER_TOPK - 1:
                thr = mx
            else:
                cur = [[jnp.where(x == mm, NEG_INF, x) for x in cs] for cs, mm in zip(cur, mx)]
        for u, c in enumerate(cols):
            z = reduce_all([jnp.where(x >= thr[u], jnp.exp2(x), 0.0) for x in cands[u]], jnp.add, jnp.sum)
            logz = jnp.log2(z)
            shifted = pair_sums([r - logz for r in a_rows[u]], a_all[u] - logz, b_all[u], b_rows[u])
            tz = reduce_all([jnp.where(x >= thr[u], y, POS_INF) for x, y in zip(cands[u], shifted)],
                            jnp.minimum, jnp.min)
            thr_ref[hh, :, c] = jnp.broadcast_to(tz, (8, 128))
            s_ref[2 * hh, :, c] = (s_ref[2 * hh, :, c] - m[u]) - logz
        return carry
    lax.fori_loop(0, PEER_HEADS * n_grp, cand_body, 0)


def _peer_kernel(h_ref, g_ref, wqt_ref, sk_ref, u_ref, vt_ref, fg_ref, *refs, n_e, group_tiles):
    n_out = len(group_tiles) if group_tiles else 1
    out_refs = refs[:n_out]
    hnt_ref, qt_ref, s_ref, top_ref, thr_ref, s1b_ref, act0_ref, act1_ref, d0_ref, d1_ref, yt_ref = refs[n_out:]
    _peer_body(h_ref, g_ref, wqt_ref, sk_ref, u_ref, vt_ref, fg_ref, out_refs,
               hnt_ref, qt_ref, s_ref, top_ref, thr_ref, s1b_ref, act0_ref, act1_ref, d0_ref, d1_ref, yt_ref,
               n_e=n_e, group_tiles=group_tiles)


def _peer_body(h_ref, g_ref, wqt_ref, sk_ref, u_ref, vt_ref, fg_ref, out_refs,
               hnt_ref, qt_ref, s_ref, top_ref, thr_ref, s1b_ref, act0_ref, act1_ref, d0_ref, d1_ref, yt_ref,
               *, n_e, group_tiles):
    e = pl.program_id(1)
    tb = h_ref.shape[0]
    n_tc = tb // 128
    nt = u_ref.shape[0]
    n_r = nt // 128

    @pl.when(e == 0)
    def _prologue():
        _peer_route(h_ref, g_ref, wqt_ref, sk_ref, hnt_ref, qt_ref, s_ref, top_ref, thr_ref)
        act0_ref[...] = jnp.dot(u_ref[...], hnt_ref[...], preferred_element_type=F32)
        d1_ref[...] = jnp.zeros_like(d1_ref)
        yt_ref[...] = jnp.zeros_like(yt_ref)

    def body(act_in, act_out, d_in, d_out):
        d_model = vt_ref.shape[0]
        n_q = d_model // Y_ROWS
        k_q = d_model // n_q

        i1_0 = jnp.minimum((e - 1) * n_r, N_KEYS - n_r)
        grp0 = pl.multiple_of((i1_0 // 8) * 8, 8)
        sub_iota = lax.broadcasted_iota(jnp.int32, (8, tb), 0)
        for hh in range(PEER_HEADS):
            grp = s_ref[2 * hh, pl.ds(grp0, 8), :]
            for r in range(n_r):
                row = jnp.sum(jnp.where(sub_iota == i1_0 % 8 + r, grp, 0.0), axis=0, keepdims=True)
                s1b_ref[hh, r] = jnp.broadcast_to(row, (8, tb))

        def d_chunk(r, tc, sub):
            keys = slice(sub * D_SUB, (sub + 1) * D_SUB)
            rows = slice(r * 128 + sub * D_SUB, r * 128 + (sub + 1) * D_SUB)
            cols = slice(tc * 128, (tc + 1) * 128)
            w = jnp.zeros((D_SUB // 8, 8, 128), F32)
            for hh in range(PEER_HEADS):
                s2 = s_ref[2 * hh + 1, keys, cols].reshape(D_SUB // 8, 8, 128)
                x = s2 + s1b_ref[hh, r, :, cols][None]
                w = w + jnp.where(x >= thr_ref[hh, :, cols][None], jnp.exp2(x), 0.0)
            a = act_in[rows, cols]
            gl = 0.5 * a * (1.0 + lax.erf(a * (2.0 ** -0.5)))
            d_out[rows, cols] = (gl * w.reshape(D_SUB, 128)).astype(BF16)

        chunks = [(r, tc, sub) for r in range(n_r) for tc in range(n_tc) for sub in range(128 // D_SUB)]
        per_tick = len(chunks) // (n_q * (tb // MXU_N))
        ci = 0
        for c in range(tb // MXU_N):
            cb = slice(c * MXU_N, (c + 1) * MXU_N)
            acc = None
            for kq in range(n_q):
                kb = slice(kq * k_q, (kq + 1) * k_q)
                part = jnp.dot(u_ref[:, kb], hnt_ref[kb, cb], preferred_element_type=F32)
                acc = part if acc is None else acc + part
                for _ in range(per_tick // 2):
                    d_chunk(*chunks[ci]); ci += 1
                rb = slice(kq * Y_ROWS, (kq + 1) * Y_ROWS)
                yt_ref[rb, cb] += jnp.dot(vt_ref[rb, :], d_in[:, cb], preferred_element_type=F32)
                for _ in range(per_tick - per_tick // 2):
                    d_chunk(*chunks[ci]); ci += 1
            act_out[:, cb] = acc
        assert ci == len(chunks)

    @pl.when(jnp.logical_and(e > 0, e % 2 == 1))
    def _odd():
        body(act0_ref, act1_ref, d1_ref, d0_ref)

    @pl.when(jnp.logical_and(e > 0, e % 2 == 0))
    def _even():
        body(act1_ref, act0_ref, d0_ref, d1_ref)

    @pl.when(e == n_e + 1)
    def _finish():
        y = h_ref[...] + yt_ref[...].T
        if not group_tiles:
            out_refs[0][...] = y
            return
        y = _rms(y, fg_ref[...])
        t = pl.program_id(0)
        lo = 0
        for o_ref, n in zip(out_refs, group_tiles):
            @pl.when(jnp.logical_and(t >= lo, t < lo + n))
            def _store(o_ref=o_ref):
                o_ref[...] = y
            lo += n


def _peer(h, g, wqt_bf, sk_bf, u_bf, vt_bf, fg, group_tiles=None):
    tp, d = h.shape
    ne = u_bf.shape[0]
    tb, nt = PEER_TB, PEER_NT
    n_e = ne // nt
    assert n_e % 2 == 0
    if group_tiles:
        n_tiles = sum(group_tiles)
        starts = [sum(group_tiles[:k]) for k in range(len(group_tiles))]
        out_specs = [pl.BlockSpec((tb, d), lambda t, e, lo=lo, n=n: (jnp.clip(t - lo, 0, n - 1), 0),
                                  pipeline_mode=pl.Buffered(1))
                     for lo, n in zip(starts, group_tiles)]
        out_shape = [jax.ShapeDtypeStruct((n * tb, d), F32) for n in group_tiles]
    else:
        n_tiles = tp // tb
        out_specs = pl.BlockSpec((tb, d), lambda t, e: (t, 0))
        out_shape = jax.ShapeDtypeStruct((tp, d), F32)
    return pl.pallas_call(
        functools.partial(_peer_kernel, n_e=n_e, group_tiles=group_tiles),
        grid=(n_tiles, n_e + 2),
        in_specs=[pl.BlockSpec((tb, d), lambda t, e: (t, 0), pipeline_mode=pl.Buffered(1)),
                  pl.BlockSpec((1, d), lambda t, e: (0, 0)),
                  pl.BlockSpec(wqt_bf.shape, lambda t, e: (0, 0), pipeline_mode=pl.Buffered(1)),
                  pl.BlockSpec(sk_bf.shape, lambda t, e: (0, 0, 0)),
                  pl.BlockSpec((nt, d), lambda t, e: (jnp.minimum(e, n_e - 1), 0)),
                  pl.BlockSpec((d, nt), lambda t, e: (0, jnp.clip(e - 2, 0, n_e - 1))),
                  pl.BlockSpec((1, d), lambda t, e: (0, 0))],
        out_specs=out_specs,
        out_shape=out_shape,
        scratch_shapes=[pltpu.VMEM((d, tb), BF16),
                        pltpu.VMEM((wqt_bf.shape[0], tb), BF16),
                        pltpu.VMEM((2 * PEER_HEADS, N_KEYS, tb), F32),
                        pltpu.VMEM((2 * PEER_HEADS, PEER_TOPK, tb), F32),
                        pltpu.VMEM((PEER_HEADS, 8, tb), F32),
                        pltpu.VMEM((PEER_HEADS, nt // 128, 8, tb), F32),
                        pltpu.VMEM((nt, tb), F32),
                        pltpu.VMEM((nt, tb), F32),
                        pltpu.VMEM((nt, tb), BF16),
                        pltpu.VMEM((nt, tb), BF16),
                        pltpu.VMEM((d, tb), F32)],
        compiler_params=pltpu.CompilerParams(dimension_semantics=("arbitrary", "arbitrary"),
                                             vmem_limit_bytes=VMEM_LIMIT),
    )(h, g, wqt_bf, sk_bf, u_bf, vt_bf, fg)


def _schedule(seq_lens):
    n_real = [s // BLK for s in seq_lens]
    starts = [sum(n_real[:k]) for k in range(len(n_real))]
    meta0 = sum(n_real)
    n_used = meta0 + len(seq_lens)
    per = max(PEER_TB, IN_TM) // BLK
    nblk = -(-(n_used + 1) // per) * per
    zero = nblk - 1
    jpos, nbs, prv, nxt, mta, fvr = [], [], [], [], [], []
    for k, n in enumerate(n_real):
        for r in range(n):
            i = starts[k] + r
            jpos.append(r + 1); nbs.append(n + 1); mta.append(meta0 + k); fvr.append(0)
            prv.append(meta0 + k if r == 0 else i - 1)
            nxt.append(zero if r == n - 1 else i + 1)
    for k, n in enumerate(n_real):
        jpos.append(0); nbs.append(n + 1); mta.append(meta0 + k); fvr.append(META_ROW0)
        prv.append(zero); nxt.append(starts[k])
    for i in range(n_used, nblk):
        jpos.append(0); nbs.append(1); mta.append(i); fvr.append(BLK)
        prv.append(zero); nxt.append(zero)
    tables = tuple(jnp.asarray(np.asarray(a, np.int32)) for a in (jpos, nbs, prv, nxt, mta, fvr))
    return tables, nblk


def kernel(x_prompt, x_sample, meta_tokens, ln1_g, w_in, conv_a_w, attn_sink, conf_dw_w, conf_dw_b, conf_ln_g,
           conf_ln_b, g_out_a, g_out_b, g_out_c, w_out, ln2_g, peer_wq, peer_subkeys, peer_u, peer_v, final_g):
    d = x_prompt.shape[-1]
    depth = w_in.shape[0]
    groups = (x_prompt, x_sample)
    seq_lens = [x.shape[1] for x in groups for _ in range(x.shape[0])]
    group_rows = [x.shape[0] * x.shape[1] for x in groups]
    assert all(s % BLK == 0 for s in seq_lens) and all(r % PEER_TB == 0 for r in group_rows)
    sched, nblk = _schedule(seq_lens)
    lead = jnp.concatenate([jnp.zeros((META_ROW0, d), F32), meta_tokens.astype(F32)], axis=0)
    n_tail = nblk * BLK - sum(group_rows) - BLK * len(seq_lens)
    h = jnp.concatenate([x.reshape(-1, d) for x in groups] + [lead] * len(seq_lens) + [jnp.zeros((n_tail, d), F32)],
                        axis=0)

    row = lambda a: a.reshape(1, -1)
    for l in range(depth):
        ua, bg, q, kv, glu = _inproj(h, row(ln1_g[l]), w_in[l].astype(BF16))
        h = _mixer(sched, ua, bg, q, kv, glu, h, conv_a_w[l], attn_sink[l], conf_dw_w[l], row(conf_dw_b[l]),
                   row(conf_ln_g[l]), row(conf_ln_b[l]), row(g_out_a[l]), row(g_out_b[l]), row(g_out_c[l]),
                   w_out[l].astype(BF16))
        sk = peer_subkeys[l].reshape(2 * PEER_HEADS, N_KEYS, -1).astype(BF16)
        last = l == depth - 1
        h = _peer(h, row(ln2_g[l]), peer_wq[l].T.astype(BF16), sk, peer_u[l].astype(BF16),
                  peer_v[l].T.astype(BF16), row(final_g),
                  group_tiles=tuple(r // PEER_TB for r in group_rows) if last else None)
    return tuple(y.reshape(x.shape) for y, x in zip(h, groups))
```

```python
import functools

import numpy as np
import jax
import jax.numpy as jnp
from jax import lax
from jax.experimental import pallas as pl
from jax.experimental.pallas import tpu as pltpu

F32 = jnp.float32
BF16 = jnp.bfloat16

N_META = 16
D_SCONV = 512
N_Q_HEADS = 16
N_KV_HEADS = 4
HEAD_DIM = 64
GQA_GROUP = N_Q_HEADS // N_KV_HEADS
D_ATTN = N_Q_HEADS * HEAD_DIM
D_KV = N_KV_HEADS * HEAD_DIM
D_CONF = 512
SHORT_CONV_W = 3
CONF_CONV_W = 31
WINDOW = 128
BLK = 128
META_ROW0 = BLK - N_META
N_KEYS = 128
PEER_HEADS = 8
PEER_TOPK = 16
RMS_EPS = 1e-6
LN_EPS = 1e-5
NEG = -1e30
NEG_INF = float("-inf")
POS_INF = float("inf")
LOG2E = 1.4426950408889634

IN_TM = 256
PEER_TB = 512
PEER_NT = 512
MXU_N = 256
Y_ROWS = 512
ATT_ROWS = 32
ATT_ILP = 2
ROUTE_ILP = 2
D_SUB = 64
VMEM_LIMIT = 56 * 1024 * 1024


def _rms(x, g):
    return x * lax.rsqrt(jnp.mean(x * x, axis=-1, keepdims=True) + RMS_EPS) * g


def _inproj_kernel(h_ref, g_ref, w_ref, ua_ref, bg_ref, q_ref, kv_ref, glu_ref):
    xn = _rms(h_ref[...], g_ref[...]).astype(BF16)
    z = jnp.dot(xn, w_ref[...], preferred_element_type=F32)
    o = 0
    a_x = z[:, o:o + D_SCONV]; o += D_SCONV
    a_bg = z[:, o:o + D_SCONV]; o += D_SCONV
    a_cg = z[:, o:o + D_SCONV]; o += D_SCONV
    q = z[:, o:o + D_ATTN]; o += D_ATTN
    kv = z[:, o:o + 2 * D_KV]; o += 2 * D_KV
    c_a = z[:, o:o + D_CONF]; o += D_CONF
    c_g = z[:, o:o + D_CONF]
    ua_ref[...] = a_cg * a_x
    bg_ref[...] = a_bg
    q_ref[...] = (q * (HEAD_DIM ** -0.5 * LOG2E)).astype(BF16)
    kv_ref[...] = kv.astype(BF16)
    glu_ref[...] = c_a * jax.nn.sigmoid(c_g)


def _inproj(h, g, w_bf, l):
    tp, d = h.shape
    n = w_bf.shape[2]
    row = lambda w: pl.BlockSpec((IN_TM, w), lambda i: (i, 0))
    return pl.pallas_call(
        _inproj_kernel,
        grid=(tp // IN_TM,),
        in_specs=[row(d),
                  pl.BlockSpec((1, d), lambda i: (0, 0)),
                  pl.BlockSpec((None, d, n), lambda i: (l, 0, 0), pipeline_mode=pl.Buffered(1))],
        out_specs=[row(D_SCONV), row(D_SCONV), row(D_ATTN), row(2 * D_KV), row(D_CONF)],
        out_shape=[jax.ShapeDtypeStruct((tp, D_SCONV), F32),
                   jax.ShapeDtypeStruct((tp, D_SCONV), F32),
                   jax.ShapeDtypeStruct((tp, D_ATTN), BF16),
                   jax.ShapeDtypeStruct((tp, 2 * D_KV), BF16),
                   jax.ShapeDtypeStruct((tp, D_CONF), F32)],
        compiler_params=pltpu.CompilerParams(dimension_semantics=("arbitrary",),
                                             vmem_limit_bytes=VMEM_LIMIT),
    )(h, g, w_bf)


def _mixer_kernel(jpos_ref, nbs_ref, prv_ref, nxt_ref, mta_ref, fvr_ref,
                  ua_p, ua_c, ua_n, bg_ref, q_ref, kv_p, kv_c, kv_n, kv_m, glu_p, glu_c, glu_n, h_ref,
                  cw_ref, sink_ref, dw_ref, db_ref, lg_ref, lb_ref, ga_ref, gb_ref, gc_ref, wo_ref,
                  out_ref, xa_ref, xc_ref, cc_ref, yb_ref, dist_ref, s_ref, p_ref):
    i = pl.program_id(0)
    j = jpos_ref[i]
    nb = nbs_ref[i]

    xa_ref[0:8, :] = ua_p[...]
    xa_ref[8:8 + BLK, :] = ua_c[...]
    xa_ref[8 + BLK:16 + BLK, :] = ua_n[...]
    conv_a = jnp.zeros((BLK, D_SCONV), F32)
    for k in range(SHORT_CONV_W):
        conv_a = conv_a + cw_ref[k:k + 1, :] * xa_ref[pl.ds(8 + k - SHORT_CONV_W // 2, BLK), :]
    y_a = _rms(bg_ref[...] * conv_a, ga_ref[...])

    n_grp = BLK // 8
    n_slab = D_CONF // 128
    for cs in range(n_slab):
        cols = slice(cs * 128, (cs + 1) * 128)
        xc_ref[cs, 0:16, :] = glu_p[:, cols]
        xc_ref[cs, 16:16 + BLK, :] = glu_c[:, cols]
        xc_ref[cs, 16 + BLK:32 + BLK, :] = glu_n[:, cols]
        accs = [jnp.zeros((8, 128), F32) for _ in range(n_grp)]
        for k in range(CONF_CONV_W):
            wk = jnp.broadcast_to(dw_ref[k:k + 1, cols], (8, 128))
            for g in range(n_grp):
                accs[g] = accs[g] + wk * xc_ref[cs, pl.ds(g + k + 16 - CONF_CONV_W // 2, 8, stride=n_grp), :]
        for g in range(n_grp):
            cc_ref[cs, g * 8:(g + 1) * 8, :] = accs[g] + db_ref[:, cols]
    conv_c = jnp.concatenate([cc_ref[cs] for cs in range(n_slab)], axis=1)
    mu = jnp.mean(conv_c, axis=-1, keepdims=True)
    xc = conv_c - mu
    var = jnp.mean(xc * xc, axis=-1, keepdims=True)
    ln = xc * lax.rsqrt(var + LN_EPS) * lg_ref[...] + lb_ref[...]
    y_c_strided = _rms(ln * jax.nn.sigmoid(ln), gc_ref[...])
    for cs in range(n_slab):
        cc_ref[cs] = y_c_strided[:, cs * 128:(cs + 1) * 128]
    y_c = jnp.concatenate(
        [jnp.concatenate([cc_ref[cs, pl.ds((q % 2) * (BLK // 2) + q // 2, 8, stride=8), :]
                          for cs in range(n_slab)], axis=1) for q in range(n_grp)], axis=0)

    n_keys = 4 * BLK
    rq = lax.broadcasted_iota(jnp.int32, (BLK, n_keys), 0)
    ck = lax.broadcasted_iota(jnp.int32, (BLK, n_keys), 1)
    rk = ck % BLK
    kb = ck // BLK
    jb = jnp.full((BLK, n_keys), j, jnp.int32)
    qpos = jb * BLK - META_ROW0 + rq
    dist = jnp.where(kb == 0, rq + BLK - rk,
                     jnp.where(kb == 1, jnp.abs(rq - rk),
                               jnp.where(kb == 2, BLK + rk - rq,
                                         jnp.minimum(jnp.abs(qpos - (rk - META_ROW0)), WINDOW))))
    both = jnp.logical_and
    ok = jnp.logical_or(
        jnp.logical_or(both(kb == 0, both(rk >= rq, jb >= 2)), both(kb == 1, jb >= 1)),
        jnp.logical_or(both(kb == 2, both(rk <= rq, jb <= nb - 2)), both(kb == 3, rk >= META_ROW0)))
    dist_ref[...] = jnp.where(ok, dist, -1).astype(F32)
    sink_col = lax.broadcasted_iota(jnp.int32, (ATT_ROWS, n_keys), 1) == 3 * BLK
    nt = (((1,), (1,)), ((), ()))
    lane = lax.broadcasted_iota(jnp.int32, (1, 2 * HEAD_DIM), 1)

    def value_tile(ref, g):
        slab = ref[:, D_KV + (g // 2) * 2 * HEAD_DIM:D_KV + (g // 2 + 1) * 2 * HEAD_DIM]
        own = (lane < HEAD_DIM) if g % 2 == 0 else (lane >= HEAD_DIM)
        ones_at = HEAD_DIM if g % 2 == 0 else 0
        return jnp.where(own, slab, (lane == ones_at).astype(BF16))

    for g in range(N_KV_HEADS):
        ks = slice(g * HEAD_DIM, (g + 1) * HEAD_DIM)
        heads = range(g * GQA_GROUP, (g + 1) * GQA_GROUP)
        q_grp = jnp.concatenate([q_ref[:, hd * HEAD_DIM:(hd + 1) * HEAD_DIM] for hd in heads], axis=0)
        k_all = jnp.concatenate([kv_p[:, ks], kv_c[:, ks], kv_n[:, ks], kv_m[:, ks]], axis=0)
        v_all = jnp.concatenate([value_tile(kv_p, g), value_tile(kv_c, g), value_tile(kv_n, g),
                                 value_tile(kv_m, g)], axis=0)
        s_ref[...] = lax.dot_general(q_grp, k_all, nt, preferred_element_type=F32)
        for hq, hd in enumerate(heads):
            slope = float(2.0 ** (-8.0 * (hd + 1) / N_Q_HEADS)) * LOG2E
            sink = sink_ref[hd] * LOG2E
            for r0 in range(0, BLK, ATT_ROWS * ATT_ILP):
                local = [slice(r0 + u * ATT_ROWS, r0 + (u + 1) * ATT_ROWS) for u in range(ATT_ILP)]
                rows = [slice(hq * BLK + lc.start, hq * BLK + lc.stop) for lc in local]
                dd = [dist_ref[lc, :] for lc in local]
                sc = [jnp.where(d >= 0.0, s_ref[rw, :] - slope * d, NEG) for rw, d in zip(rows, dd)]
                sc = [jnp.where(sink_col, sink, x) for x in sc]
                mx = [jnp.max(jnp.maximum(jnp.maximum(x[:, 0:BLK], x[:, BLK:2 * BLK]),
                                          jnp.maximum(x[:, 2 * BLK:3 * BLK], x[:, 3 * BLK:])),
                              axis=-1, keepdims=True) for x in sc]
                for rw, x, m in zip(rows, sc, mx):
                    p_ref[rw, :] = jnp.exp2(x - m).astype(BF16)
        o_all = jnp.dot(p_ref[...], v_all, preferred_element_type=F32)
        o_lo = 0 if g % 2 == 0 else HEAD_DIM
        sum_at = HEAD_DIM if g % 2 == 0 else 0
        for hq, hd in enumerate(heads):
            o = o_all[hq * BLK:(hq + 1) * BLK]
            yb_ref[:, hd * HEAD_DIM:(hd + 1) * HEAD_DIM] = o[:, o_lo:o_lo + HEAD_DIM] / o[:, sum_at:sum_at + 1]
    y_b = _rms(yb_ref[...], gb_ref[...])

    out = (jnp.dot(y_a.astype(BF16), wo_ref[0:D_SCONV, :], preferred_element_type=F32)
           + jnp.dot(y_b.astype(BF16), wo_ref[D_SCONV:D_SCONV + D_ATTN, :], preferred_element_type=F32)
           + jnp.dot(y_c.astype(BF16), wo_ref[D_SCONV + D_ATTN:, :], preferred_element_type=F32))
    keep = lax.broadcasted_iota(jnp.int32, (BLK, 1), 0) >= fvr_ref[i]
    out_ref[...] = jnp.where(keep, h_ref[...] + out, 0.0)


def _mixer(sched, ua, bg, q, kv, glu, h, cw, sink, dw, db, lg, lb, ga, gb, gc, wo_bf, l):
    tp, d = h.shape
    nblk = tp // BLK
    cur = lambda w: pl.BlockSpec((BLK, w), lambda i, *_: (i, 0))
    prv = lambda w: pl.BlockSpec((BLK, w), lambda i, jp, nb, pv, nx, mt, fv: (pv[i], 0))
    nxt = lambda w: pl.BlockSpec((BLK, w), lambda i, jp, nb, pv, nx, mt, fv: (nx[i], 0))
    halo_p = lambda r, w: pl.BlockSpec((r, w), lambda i, jp, nb, pv, nx, mt, fv: ((pv[i] + 1) * (BLK // r) - 1, 0))
    halo_n = lambda r, w: pl.BlockSpec((r, w), lambda i, jp, nb, pv, nx, mt, fv: (nx[i] * (BLK // r), 0))
    meta = lambda w: pl.BlockSpec((BLK, w), lambda i, jp, nb, pv, nx, mt, fv: (mt[i], 0))
    full = lambda a: pl.BlockSpec(a.shape, lambda i, *_: (0,) * a.ndim)
    grid_spec = pltpu.PrefetchScalarGridSpec(
        num_scalar_prefetch=len(sched),
        grid=(nblk,),
        in_specs=[halo_p(8, D_SCONV), cur(D_SCONV), halo_n(8, D_SCONV), cur(D_SCONV), cur(D_ATTN),
                  prv(2 * D_KV), cur(2 * D_KV), nxt(2 * D_KV), meta(2 * D_KV),
                  halo_p(16, D_CONF), cur(D_CONF), halo_n(16, D_CONF), cur(d),
                  full(cw), pl.BlockSpec(memory_space=pltpu.SMEM), full(dw), full(db), full(lg), full(lb),
                  full(ga), full(gb), full(gc),
                  pl.BlockSpec((None,) + wo_bf.shape[1:], lambda i, *_: (l, 0, 0), pipeline_mode=pl.Buffered(1))],
        out_specs=cur(d),
        scratch_shapes=[pltpu.VMEM((BLK + 16, D_SCONV), F32),
                        pltpu.VMEM((D_CONF // 128, BLK + 32, 128), F32),
                        pltpu.VMEM((D_CONF // 128, BLK, 128), F32),
                        pltpu.VMEM((BLK, D_ATTN), F32),
                        pltpu.VMEM((BLK, 4 * BLK), F32),
                        pltpu.VMEM((GQA_GROUP * BLK, 4 * BLK), F32),
                        pltpu.VMEM((GQA_GROUP * BLK, 4 * BLK), BF16)],
    )
    return pl.pallas_call(
        _mixer_kernel,
        grid_spec=grid_spec,
        out_shape=jax.ShapeDtypeStruct((tp, d), F32),
        compiler_params=pltpu.CompilerParams(dimension_semantics=("arbitrary",),
                                             vmem_limit_bytes=VMEM_LIMIT),
    )(*sched, ua, ua, ua, bg, q, kv, kv, kv, kv, glu, glu, glu, h,
      cw, sink, dw, db, lg, lb, ga, gb, gc, wo_bf)


def _peer_route(h_ref, g_ref, wqt_ref, sk_ref, hnt_ref, qt_ref, s_ref, top_ref, thr_ref):
    tb = h_ref.shape[0]
    n_tc = tb // 128
    xn = _rms(h_ref[...], g_ref[...])
    hnt_ref[...] = xn.T.astype(BF16)
    qt_ref[...] = jnp.dot(wqt_ref[...], hnt_ref[...], preferred_element_type=F32).astype(BF16)

    def score_body(hc, carry):
        row0 = pl.multiple_of(hc * 128, 128)
        s_ref[hc] = jnp.dot(sk_ref[hc], qt_ref[pl.ds(row0, 128), :], preferred_element_type=F32) * LOG2E
        return carry
    lax.fori_loop(0, 2 * PEER_HEADS, score_body, 0)

    n_grp = n_tc // ROUTE_ILP

    def chunk_cols(it):
        return [pl.ds(pl.multiple_of(((it % n_grp) * ROUTE_ILP + u) * 128, 128), 128) for u in range(ROUTE_ILP)]

    def top_body(it, carry):
        hc = it // n_grp
        cols = chunk_cols(it)
        cur = [s_ref[hc, :, c] for c in cols]
        for k in range(PEER_TOPK):
            m = [jnp.max(x, axis=0, keepdims=True) for x in cur]
            for c, mm in zip(cols, m):
                top_ref[hc, k:k + 1, c] = mm
            cur = [jnp.where(x == mm, NEG_INF, x) for x, mm in zip(cur, m)]
        return carry
    lax.fori_loop(0, 2 * PEER_HEADS * n_grp, top_body, 0)

    row8 = lax.broadcasted_iota(jnp.int32, (8, 128), 0)
    row16 = lax.broadcasted_iota(jnp.int32, (PEER_TOPK, 128), 0)
    n_lead = 4

    def pair_sums(a_rows, a_all, b_all, b_rows):
        out = [a_rows[0] + b_all]
        for i in range(1, n_lead):
            out.append(jnp.where(row8 < PEER_TOPK // (i + 1), a_rows[i] + b_all[0:8], NEG_INF))
        out.append(jnp.where(row16 >= n_lead, a_all + b_rows[0], NEG_INF))
        for j in range(1, PEER_TOPK // (n_lead + 1)):
            out.append(jnp.where(jnp.logical_and(row8 >= n_lead, row8 < PEER_TOPK // (j + 1)),
                                 a_all[0:8] + b_rows[j], NEG_INF))
        return out

    def fold8(x, op):
        return x if x.shape[0] == 8 else op(x[0:8], x[8:16])

    def reduce_all(xs, op, red):
        acc = fold8(xs[0], op)
        for x in xs[1:]:
            acc = op(acc, fold8(x, op))
        return red(acc, axis=0, keepdims=True)

    def cand_body(it, carry):
        hh = it // n_grp
        cols = chunk_cols(it)
        n_b = PEER_TOPK // (n_lead + 1)
        b_all = [top_ref[2 * hh + 1, :, c] for c in cols]
        b_rows = [[top_ref[2 * hh + 1, j:j + 1, c] for j in range(n_b)] for c in cols]
        m = [top_ref[2 * hh, 0:1, c] + br[0] for c, br in zip(cols, b_rows)]
        a_all = [top_ref[2 * hh, :, c] - mm for c, mm in zip(cols, m)]
        a_rows = [[top_ref[2 * hh, i:i + 1, c] - mm for i in range(n_lead)] for c, mm in zip(cols, m)]
        cands = [pair_sums(ar, aa, ba, br) for ar, aa, ba, br in zip(a_rows, a_all, b_all, b_rows)]
        cur = [list(cs) for cs in cands]
        thr = None
        for k in range(PEER_TOPK):
            mx = [reduce_all(cs, jnp.maximum, jnp.max) for cs in cur]
            if k == PEER_TOPK - 1:
                thr = mx
            else:
                cur = [[jnp.where(x == mm, NEG_INF, x) for x in cs] for cs, mm in zip(cur, mx)]
        for u, c in enumerate(cols):
            z = reduce_all([jnp.where(x >= thr[u], jnp.exp2(x), 0.0) for x in cands[u]], jnp.add, jnp.sum)
            logz = jnp.log2(z)
            shifted = pair_sums([r - logz for r in a_rows[u]], a_all[u] - logz, b_all[u], b_rows[u])
            tz = reduce_all([jnp.where(x >= thr[u], y, POS_INF) for x, y in zip(cands[u], shifted)],
                            jnp.minimum, jnp.min)
            thr_ref[hh, :, c] = jnp.broadcast_to(tz, (8, 128))
            s_ref[2 * hh, :, c] = (s_ref[2 * hh, :, c] - m[u]) - logz
        return carry
    lax.fori_loop(0, PEER_HEADS * n_grp, cand_body, 0)


def _peer_kernel(h_ref, g_ref, wqt_ref, sk_ref, u_ref, vt_ref, fg_ref, *refs, n_e, group_tiles):
    n_out = len(group_tiles) if group_tiles else 1
    out_refs = refs[:n_out]
    hnt_ref, qt_ref, s_ref, top_ref, thr_ref, s1b_ref, act0_ref, act1_ref, d0_ref, d1_ref, yt_ref = refs[n_out:]
    _peer_body(h_ref, g_ref, wqt_ref, sk_ref, u_ref, vt_ref, fg_ref, out_refs,
               hnt_ref, qt_ref, s_ref, top_ref, thr_ref, s1b_ref, act0_ref, act1_ref, d0_ref, d1_ref, yt_ref,
               n_e=n_e, group_tiles=group_tiles)


def _peer_body(h_ref, g_ref, wqt_ref, sk_ref, u_ref, vt_ref, fg_ref, out_refs,
               hnt_ref, qt_ref, s_ref, top_ref, thr_ref, s1b_ref, act0_ref, act1_ref, d0_ref, d1_ref, yt_ref,
               *, n_e, group_tiles):
    e = pl.program_id(1)
    tb = h_ref.shape[0]
    n_tc = tb // 128
    nt = u_ref.shape[0]
    n_r = nt // 128

    @pl.when(e == 0)
    def _prologue():
        _peer_route(h_ref, g_ref, wqt_ref, sk_ref, hnt_ref, qt_ref, s_ref, top_ref, thr_ref)
        act0_ref[...] = jnp.dot(u_ref[...], hnt_ref[...], preferred_element_type=F32)
        d1_ref[...] = jnp.zeros_like(d1_ref)
        yt_ref[...] = jnp.zeros_like(yt_ref)

    def body(act_in, act_out, d_in, d_out):
        d_model = vt_ref.shape[0]
        n_q = d_model // Y_ROWS
        k_q = d_model // n_q

        i1_0 = jnp.minimum((e - 1) * n_r, N_KEYS - n_r)
        grp0 = pl.multiple_of((i1_0 // 8) * 8, 8)
        sub_iota = lax.broadcasted_iota(jnp.int32, (8, tb), 0)
        for hh in range(PEER_HEADS):
            grp = s_ref[2 * hh, pl.ds(grp0, 8), :]
            for r in range(n_r):
                row = jnp.sum(jnp.where(sub_iota == i1_0 % 8 + r, grp, 0.0), axis=0, keepdims=True)
                s1b_ref[hh, r] = jnp.broadcast_to(row, (8, tb))

        def d_chunk(r, tc, sub):
            keys = slice(sub * D_SUB, (sub + 1) * D_SUB)
            rows = slice(r * 128 + sub * D_SUB, r * 128 + (sub + 1) * D_SUB)
            cols = slice(tc * 128, (tc + 1) * 128)
            w = jnp.zeros((D_SUB // 8, 8, 128), F32)
            for hh in range(PEER_HEADS):
                s2 = s_ref[2 * hh + 1, keys, cols].reshape(D_SUB // 8, 8, 128)
                x = s2 + s1b_ref[hh, r, :, cols][None]
                w = w + jnp.where(x >= thr_ref[hh, :, cols][None], jnp.exp2(x), 0.0)
            a = act_in[rows, cols]
            gl = 0.5 * a * (1.0 + lax.erf(a * (2.0 ** -0.5)))
            d_out[rows, cols] = (gl * w.reshape(D_SUB, 128)).astype(BF16)

        chunks = [(r, tc, sub) for r in range(n_r) for tc in range(n_tc) for sub in range(128 // D_SUB)]
        per_tick = len(chunks) // (n_q * (tb // MXU_N))
        ci = 0
        for c in range(tb // MXU_N):
            cb = slice(c * MXU_N, (c + 1) * MXU_N)
            acc = None
            for kq in range(n_q):
                kb = slice(kq * k_q, (kq + 1) * k_q)
                part = jnp.dot(u_ref[:, kb], hnt_ref[kb, cb], preferred_element_type=F32)
                acc = part if acc is None else acc + part
                for _ in range(per_tick // 2):
                    d_chunk(*chunks[ci]); ci += 1
                rb = slice(kq * Y_ROWS, (kq + 1) * Y_ROWS)
                yt_ref[rb, cb] += jnp.dot(vt_ref[rb, :], d_in[:, cb], preferred_element_type=F32)
                for _ in range(per_tick - per_tick // 2):
                    d_chunk(*chunks[ci]); ci += 1
            act_out[:, cb] = acc
        assert ci == len(chunks)

    @pl.when(jnp.logical_and(e > 0, e % 2 == 1))
    def _odd():
        body(act0_ref, act1_ref, d1_ref, d0_ref)

    @pl.when(jnp.logical_and(e > 0, e % 2 == 0))
    def _even():
        body(act1_ref, act0_ref, d0_ref, d1_ref)

    @pl.when(e == n_e + 1)
    def _finish():
        y = h_ref[...] + yt_ref[...].T
        if not group_tiles:
            out_refs[0][...] = y
            return
        y = _rms(y, fg_ref[...])
        t = pl.program_id(0)
        lo = 0
        for o_ref, n in zip(out_refs, group_tiles):
            @pl.when(jnp.logical_and(t >= lo, t < lo + n))
            def _store(o_ref=o_ref):
                o_ref[...] = y
            lo += n


def _peer(h, g, wqt_bf, sk_bf, u_bf, vt_bf, fg, l, group_tiles=None):
    tp, d = h.shape
    ne = u_bf.shape[1]
    tb, nt = PEER_TB, PEER_NT
    n_e = ne // nt
    assert n_e % 2 == 0
    if group_tiles:
        n_tiles = sum(group_tiles)
        starts = [sum(group_tiles[:k]) for k in range(len(group_tiles))]
        out_specs = [pl.BlockSpec((tb, d), lambda t, e, lo=lo, n=n: (jnp.clip(t - lo, 0, n - 1), 0),
                                  pipeline_mode=pl.Buffered(1))
                     for lo, n in zip(starts, group_tiles)]
        out_shape = [jax.ShapeDtypeStruct((n * tb, d), F32) for n in group_tiles]
    else:
        n_tiles = tp // tb
        out_specs = pl.BlockSpec((tb, d), lambda t, e: (t, 0))
        out_shape = jax.ShapeDtypeStruct((tp, d), F32)
    return pl.pallas_call(
        functools.partial(_peer_kernel, n_e=n_e, group_tiles=group_tiles),
        grid=(n_tiles, n_e + 2),
        in_specs=[pl.BlockSpec((tb, d), lambda t, e: (t, 0), pipeline_mode=pl.Buffered(1)),
                  pl.BlockSpec((1, d), lambda t, e: (0, 0)),
                  pl.BlockSpec((None,) + wqt_bf.shape[1:], lambda t, e: (l, 0, 0), pipeline_mode=pl.Buffered(1)),
                  pl.BlockSpec((None,) + sk_bf.shape[1:], lambda t, e: (l, 0, 0, 0)),
                  pl.BlockSpec((None, nt, d), lambda t, e: (l, jnp.minimum(e, n_e - 1), 0)),
                  pl.BlockSpec((None, d, nt), lambda t, e: (l, 0, jnp.clip(e - 2, 0, n_e - 1))),
                  pl.BlockSpec((1, d), lambda t, e: (0, 0))],
        out_specs=out_specs,
        out_shape=out_shape,
        scratch_shapes=[pltpu.VMEM((d, tb), BF16),
                        pltpu.VMEM((wqt_bf.shape[1], tb), BF16),
                        pltpu.VMEM((2 * PEER_HEADS, N_KEYS, tb), F32),
                        pltpu.VMEM((2 * PEER_HEADS, PEER_TOPK, tb), F32),
                        pltpu.VMEM((PEER_HEADS, 8, tb), F32),
                        pltpu.VMEM((PEER_HEADS, nt // 128, 8, tb), F32),
                        pltpu.VMEM((nt, tb), F32),
                        pltpu.VMEM((nt, tb), F32),
                        pltpu.VMEM((nt, tb), BF16),
                        pltpu.VMEM((nt, tb), BF16),
                        pltpu.VMEM((d, tb), F32)],
        compiler_params=pltpu.CompilerParams(dimension_semantics=("arbitrary", "arbitrary"),
                                             vmem_limit_bytes=VMEM_LIMIT),
    )(h, g, wqt_bf, sk_bf, u_bf, vt_bf, fg)


def _schedule(seq_lens):
    n_real = [s // BLK for s in seq_lens]
    starts = [sum(n_real[:k]) for k in range(len(n_real))]
    meta0 = sum(n_real)
    n_used = meta0 + len(seq_lens)
    per = max(PEER_TB, IN_TM) // BLK
    nblk = -(-(n_used + 1) // per) * per
    zero = nblk - 1
    jpos, nbs, prv, nxt, mta, fvr = [], [], [], [], [], []
    for k, n in enumerate(n_real):
        for r in range(n):
            i = starts[k] + r
            jpos.append(r + 1); nbs.append(n + 1); mta.append(meta0 + k); fvr.append(0)
            prv.append(meta0 + k if r == 0 else i - 1)
            nxt.append(zero if r == n - 1 else i + 1)
    for k, n in enumerate(n_real):
        jpos.append(0); nbs.append(n + 1); mta.append(meta0 + k); fvr.append(META_ROW0)
        prv.append(zero); nxt.append(starts[k])
    for i in range(n_used, nblk):
        jpos.append(0); nbs.append(1); mta.append(i); fvr.append(BLK)
        prv.append(zero); nxt.append(zero)
    tables = tuple(jnp.asarray(np.asarray(a, np.int32)) for a in (jpos, nbs, prv, nxt, mta, fvr))
    return tables, nblk


def kernel(x_prompt, x_sample, meta_tokens, ln1_g, w_in, conv_a_w, attn_sink, conf_dw_w, conf_dw_b, conf_ln_g,
           conf_ln_b, g_out_a, g_out_b, g_out_c, w_out, ln2_g, peer_wq, peer_subkeys, peer_u, peer_v, final_g):
    d = x_prompt.shape[-1]
    depth = w_in.shape[0]
    groups = (x_prompt, x_sample)
    seq_lens = [x.shape[1] for x in groups for _ in range(x.shape[0])]
    group_rows = [x.shape[0] * x.shape[1] for x in groups]
    assert all(s % BLK == 0 for s in seq_lens) and all(r % PEER_TB == 0 for r in group_rows)
    sched, nblk = _schedule(seq_lens)
    lead = jnp.concatenate([jnp.zeros((META_ROW0, d), F32), meta_tokens.astype(F32)], axis=0)
    n_tail = nblk * BLK - sum(group_rows) - BLK * len(seq_lens)
    h = jnp.concatenate([x.reshape(-1, d) for x in groups] + [lead] * len(seq_lens) + [jnp.zeros((n_tail, d), F32)],
                        axis=0)

    w_in_bf = w_in.astype(BF16)
    w_out_bf = w_out.astype(BF16)
    wqt_bf = jnp.swapaxes(peer_wq, 1, 2).astype(BF16)
    sk_bf = peer_subkeys.reshape(depth, 2 * PEER_HEADS, N_KEYS, -1).astype(BF16)
    u_bf = peer_u.astype(BF16)
    vt_bf = jnp.swapaxes(peer_v, 1, 2).astype(BF16)

    row = lambda a: a.reshape(1, -1)
    for l in range(depth):
        ua, bg, q, kv, glu = _inproj(h, row(ln1_g[l]), w_in_bf, l)
        h = _mixer(sched, ua, bg, q, kv, glu, h, conv_a_w[l], attn_sink[l], conf_dw_w[l], row(conf_dw_b[l]),
                   row(conf_ln_g[l]), row(conf_ln_b[l]), row(g_out_a[l]), row(g_out_b[l]), row(g_out_c[l]),
                   w_out_bf, l)
        last = l == depth - 1
        h = _peer(h, row(ln2_g[l]), wqt_bf, sk_bf, u_bf, vt_bf, row(final_g), l,
                  group_tiles=tuple(r // PEER_TB for r in group_rows) if last else None)
    return tuple(y.reshape(x.shape) for y, x in zip(h, groups))
```

```python
import functools

import numpy as np
import jax
import jax.numpy as jnp
from jax import lax
from jax.experimental import pallas as pl
from jax.experimental.pallas import tpu as pltpu

F32 = jnp.float32
BF16 = jnp.bfloat16

N_META = 16
D_SCONV = 512
N_Q_HEADS = 16
N_KV_HEADS = 4
HEAD_DIM = 64
GQA_GROUP = N_Q_HEADS // N_KV_HEADS
D_ATTN = N_Q_HEADS * HEAD_DIM
D_KV = N_KV_HEADS * HEAD_DIM
D_CONF = 512
SHORT_CONV_W = 3
CONF_CONV_W = 31
WINDOW = 128
BLK = 128
META_ROW0 = BLK - N_META
N_KEYS = 128
PEER_HEADS = 8
PEER_TOPK = 16
RMS_EPS = 1e-6
LN_EPS = 1e-5
NEG = -1e30
NEG_INF = float("-inf")
POS_INF = float("inf")
LOG2E = 1.4426950408889634

IN_TM = 256
PEER_TB = 512
PEER_NT = 512
MXU_N = 256
Y_ROWS = 512
ATT_ROWS = 32
ATT_ILP = 2
ROUTE_ILP = 2
D_SUB = 64
VMEM_LIMIT = 56 * 1024 * 1024


def _rms(x, g):
    return x * lax.rsqrt(jnp.mean(x * x, axis=-1, keepdims=True) + RMS_EPS) * g


def _inproj_kernel(h_ref, g_ref, w_ref, ua_ref, bg_ref, q_ref, kv_ref, glu_ref):
    xn = _rms(h_ref[...], g_ref[...]).astype(BF16)
    z = jnp.dot(xn, w_ref[...], preferred_element_type=F32)
    o = 0
    a_x = z[:, o:o + D_SCONV]; o += D_SCONV
    a_bg = z[:, o:o + D_SCONV]; o += D_SCONV
    a_cg = z[:, o:o + D_SCONV]; o += D_SCONV
    q = z[:, o:o + D_ATTN]; o += D_ATTN
    kv = z[:, o:o + 2 * D_KV]; o += 2 * D_KV
    c_a = z[:, o:o + D_CONF]; o += D_CONF
    c_g = z[:, o:o + D_CONF]
    ua_ref[...] = a_cg * a_x
    bg_ref[...] = a_bg
    q_ref[...] = (q * (HEAD_DIM ** -0.5 * LOG2E)).astype(BF16)
    kv_ref[...] = kv.astype(BF16)
    glu_ref[...] = c_a * jax.nn.sigmoid(c_g)


def _inproj(h, g, w_bf, l):
    tp, d = h.shape
    n = w_bf.shape[2]
    row = lambda w: pl.BlockSpec((IN_TM, w), lambda i: (i, 0))
    return pl.pallas_call(
        _inproj_kernel,
        grid=(tp // IN_TM,),
        in_specs=[row(d),
                  pl.BlockSpec((1, d), lambda i: (0, 0)),
                  pl.BlockSpec((None, d, n), lambda i: (l, 0, 0), pipeline_mode=pl.Buffered(1))],
        out_specs=[row(D_SCONV), row(D_SCONV), row(D_ATTN), row(2 * D_KV), row(D_CONF)],
        out_shape=[jax.ShapeDtypeStruct((tp, D_SCONV), F32),
                   jax.ShapeDtypeStruct((tp, D_SCONV), F32),
                   jax.ShapeDtypeStruct((tp, D_ATTN), BF16),
                   jax.ShapeDtypeStruct((tp, 2 * D_KV), BF16),
                   jax.ShapeDtypeStruct((tp, D_CONF), F32)],
        compiler_params=pltpu.CompilerParams(dimension_semantics=("arbitrary",),
                                             vmem_limit_bytes=VMEM_LIMIT),
    )(h, g, w_bf)


def _mixer_kernel(jpos_ref, nbs_ref, prv_ref, nxt_ref, mta_ref, fvr_ref,
                  ua_p, ua_c, ua_n, bg_ref, q_ref, kv_p, kv_c, kv_n, kv_m, glu_p, glu_c, glu_n, h_ref,
                  cw_ref, sink_ref, dw_ref, db_ref, lg_ref, lb_ref, ga_ref, gb_ref, gc_ref, wo_ref,
                  out_ref, xa_ref, xc_ref, cc_ref, yb_ref, dist_ref, s_ref, p_ref):
    i = pl.program_id(0)
    j = jpos_ref[i]
    nb = nbs_ref[i]

    xa_ref[0:8, :] = ua_p[...]
    xa_ref[8:8 + BLK, :] = ua_c[...]
    xa_ref[8 + BLK:16 + BLK, :] = ua_n[...]
    conv_a = jnp.zeros((BLK, D_SCONV), F32)
    for k in range(SHORT_CONV_W):
        conv_a = conv_a + cw_ref[k:k + 1, :] * xa_ref[pl.ds(8 + k - SHORT_CONV_W // 2, BLK), :]
    y_a = _rms(bg_ref[...] * conv_a, ga_ref[...])

    n_grp = BLK // 8
    n_slab = D_CONF // 128
    for cs in range(n_slab):
        cols = slice(cs * 128, (cs + 1) * 128)
        xc_ref[cs, 0:16, :] = glu_p[:, cols]
        xc_ref[cs, 16:16 + BLK, :] = glu_c[:, cols]
        xc_ref[cs, 16 + BLK:32 + BLK, :] = glu_n[:, cols]
        accs = [jnp.zeros((8, 128), F32) for _ in range(n_grp)]
        for k in range(CONF_CONV_W):
            wk = jnp.broadcast_to(dw_ref[k:k + 1, cols], (8, 128))
            for g in range(n_grp):
                accs[g] = accs[g] + wk * xc_ref[cs, pl.ds(g + k + 16 - CONF_CONV_W // 2, 8, stride=n_grp), :]
        for g in range(n_grp):
            cc_ref[cs, g * 8:(g + 1) * 8, :] = accs[g] + db_ref[:, cols]
    conv_c = jnp.concatenate([cc_ref[cs] for cs in range(n_slab)], axis=1)
    mu = jnp.mean(conv_c, axis=-1, keepdims=True)
    xc = conv_c - mu
    var = jnp.mean(xc * xc, axis=-1, keepdims=True)
    ln = xc * lax.rsqrt(var + LN_EPS) * lg_ref[...] + lb_ref[...]
    y_c_strided = _rms(ln * jax.nn.sigmoid(ln), gc_ref[...])
    for cs in range(n_slab):
        cc_ref[cs] = y_c_strided[:, cs * 128:(cs + 1) * 128]
    y_c = jnp.concatenate(
        [jnp.concatenate([cc_ref[cs, pl.ds((q % 2) * (BLK // 2) + q // 2, 8, stride=8), :]
                          for cs in range(n_slab)], axis=1) for q in range(n_grp)], axis=0)

    n_keys = 4 * BLK
    rq = lax.broadcasted_iota(jnp.int32, (BLK, n_keys), 0)
    ck = lax.broadcasted_iota(jnp.int32, (BLK, n_keys), 1)
    rk = ck % BLK
    kb = ck // BLK
    jb = jnp.full((BLK, n_keys), j, jnp.int32)
    qpos = jb * BLK - META_ROW0 + rq
    dist = jnp.where(kb == 0, rq + BLK - rk,
                     jnp.where(kb == 1, jnp.abs(rq - rk),
                               jnp.where(kb == 2, BLK + rk - rq,
                                         jnp.minimum(jnp.abs(qpos - (rk - META_ROW0)), WINDOW))))
    both = jnp.logical_and
    ok = jnp.logical_or(
        jnp.logical_or(both(kb == 0, both(rk >= rq, jb >= 2)), both(kb == 1, jb >= 1)),
        jnp.logical_or(both(kb == 2, both(rk <= rq, jb <= nb - 2)), both(kb == 3, rk >= META_ROW0)))
    dist_ref[...] = jnp.where(ok, dist, -1).astype(F32)
    sink_col = lax.broadcasted_iota(jnp.int32, (ATT_ROWS, n_keys), 1) == 3 * BLK
    nt = (((1,), (1,)), ((), ()))
    lane = lax.broadcasted_iota(jnp.int32, (1, 2 * HEAD_DIM), 1)

    def value_tile(ref, g):
        slab = ref[:, D_KV + (g // 2) * 2 * HEAD_DIM:D_KV + (g // 2 + 1) * 2 * HEAD_DIM]
        own = (lane < HEAD_DIM) if g % 2 == 0 else (lane >= HEAD_DIM)
        ones_at = HEAD_DIM if g % 2 == 0 else 0
        return jnp.where(own, slab, (lane == ones_at).astype(BF16))

    for g in range(N_KV_HEADS):
        ks = slice(g * HEAD_DIM, (g + 1) * HEAD_DIM)
        heads = range(g * GQA_GROUP, (g + 1) * GQA_GROUP)
        q_grp = jnp.concatenate([q_ref[:, hd * HEAD_DIM:(hd + 1) * HEAD_DIM] for hd in heads], axis=0)
        k_all = jnp.concatenate([kv_p[:, ks], kv_c[:, ks], kv_n[:, ks], kv_m[:, ks]], axis=0)
        v_all = jnp.concatenate([value_tile(kv_p, g), value_tile(kv_c, g), value_tile(kv_n, g),
                                 value_tile(kv_m, g)], axis=0)
        s_ref[...] = lax.dot_general(q_grp, k_all, nt, preferred_element_type=F32)
        for hq, hd in enumerate(heads):
            slope = float(2.0 ** (-8.0 * (hd + 1) / N_Q_HEADS)) * LOG2E
            sink = sink_ref[hd] * LOG2E
            for r0 in range(0, BLK, ATT_ROWS * ATT_ILP):
                local = [slice(r0 + u * ATT_ROWS, r0 + (u + 1) * ATT_ROWS) for u in range(ATT_ILP)]
                rows = [slice(hq * BLK + lc.start, hq * BLK + lc.stop) for lc in local]
                dd = [dist_ref[lc, :] for lc in local]
                sc = [jnp.where(d >= 0.0, s_ref[rw, :] - slope * d, NEG) for rw, d in zip(rows, dd)]
                sc = [jnp.where(sink_col, sink, x) for x in sc]
                mx = [jnp.max(jnp.maximum(jnp.maximum(x[:, 0:BLK], x[:, BLK:2 * BLK]),
                                          jnp.maximum(x[:, 2 * BLK:3 * BLK], x[:, 3 * BLK:])),
                              axis=-1, keepdims=True) for x in sc]
                for rw, x, m in zip(rows, sc, mx):
                    p_ref[rw, :] = jnp.exp2(x - m).astype(BF16)
        o_all = jnp.dot(p_ref[...], v_all, preferred_element_type=F32)
        o_lo = 0 if g % 2 == 0 else HEAD_DIM
        sum_at = HEAD_DIM if g % 2 == 0 else 0
        for hq, hd in enumerate(heads):
            o = o_all[hq * BLK:(hq + 1) * BLK]
            yb_ref[:, hd * HEAD_DIM:(hd + 1) * HEAD_DIM] = o[:, o_lo:o_lo + HEAD_DIM] / o[:, sum_at:sum_at + 1]
    y_b = _rms(yb_ref[...], gb_ref[...])

    out = (jnp.dot(y_a.astype(BF16), wo_ref[0:D_SCONV, :], preferred_element_type=F32)
           + jnp.dot(y_b.astype(BF16), wo_ref[D_SCONV:D_SCONV + D_ATTN, :], preferred_element_type=F32)
           + jnp.dot(y_c.astype(BF16), wo_ref[D_SCONV + D_ATTN:, :], preferred_element_type=F32))
    keep = lax.broadcasted_iota(jnp.int32, (BLK, 1), 0) >= fvr_ref[i]
    out_ref[...] = jnp.where(keep, h_ref[...] + out, 0.0)


def _mixer(sched, ua, bg, q, kv, glu, h, cw, sink, dw, db, lg, lb, ga, gb, gc, wo_bf, l):
    tp, d = h.shape
    nblk = tp // BLK
    cur = lambda w: pl.BlockSpec((BLK, w), lambda i, *_: (i, 0))
    prv = lambda w: pl.BlockSpec((BLK, w), lambda i, jp, nb, pv, nx, mt, fv: (pv[i], 0))
    nxt = lambda w: pl.BlockSpec((BLK, w), lambda i, jp, nb, pv, nx, mt, fv: (nx[i], 0))
    halo_p = lambda r, w: pl.BlockSpec((r, w), lambda i, jp, nb, pv, nx, mt, fv: ((pv[i] + 1) * (BLK // r) - 1, 0))
    halo_n = lambda r, w: pl.BlockSpec((r, w), lambda i, jp, nb, pv, nx, mt, fv: (nx[i] * (BLK // r), 0))
    meta = lambda w: pl.BlockSpec((BLK, w), lambda i, jp, nb, pv, nx, mt, fv: (mt[i], 0))
    full = lambda a: pl.BlockSpec(a.shape, lambda i, *_: (0,) * a.ndim)
    grid_spec = pltpu.PrefetchScalarGridSpec(
        num_scalar_prefetch=len(sched),
        grid=(nblk,),
        in_specs=[halo_p(8, D_SCONV), cur(D_SCONV), halo_n(8, D_SCONV), cur(D_SCONV), cur(D_ATTN),
                  prv(2 * D_KV), cur(2 * D_KV), nxt(2 * D_KV), meta(2 * D_KV),
                  halo_p(16, D_CONF), cur(D_CONF), halo_n(16, D_CONF), cur(d),
                  full(cw), pl.BlockSpec(memory_space=pltpu.SMEM), full(dw), full(db), full(lg), full(lb),
                  full(ga), full(gb), full(gc),
                  pl.BlockSpec((None,) + wo_bf.shape[1:], lambda i, *_: (l, 0, 0), pipeline_mode=pl.Buffered(1))],
        out_specs=cur(d),
        scratch_shapes=[pltpu.VMEM((BLK + 16, D_SCONV), F32),
                        pltpu.VMEM((D_CONF // 128, BLK + 32, 128), F32),
                        pltpu.VMEM((D_CONF // 128, BLK, 128), F32),
                        pltpu.VMEM((BLK, D_ATTN), F32),
                        pltpu.VMEM((BLK, 4 * BLK), F32),
                        pltpu.VMEM((GQA_GROUP * BLK, 4 * BLK), F32),
                        pltpu.VMEM((GQA_GROUP * BLK, 4 * BLK), BF16)],
    )
    return pl.pallas_call(
        _mixer_kernel,
        grid_spec=grid_spec,
        out_shape=jax.ShapeDtypeStruct((tp, d), F32),
        compiler_params=pltpu.CompilerParams(dimension_semantics=("arbitrary",),
                                             vmem_limit_bytes=VMEM_LIMIT),
    )(*sched, ua, ua, ua, bg, q, kv, kv, kv, kv, glu, glu, glu, h,
      cw, sink, dw, db, lg, lb, ga, gb, gc, wo_bf)


def _peer_route(h_ref, g_ref, wqt_ref, sk_ref, hnt_ref, qt_ref, s_ref, top_ref, thr_ref):
    tb = h_ref.shape[0]
    n_tc = tb // 128
    xn = _rms(h_ref[...], g_ref[...])
    hnt_ref[...] = xn.T.astype(BF16)
    qt_ref[...] = jnp.dot(wqt_ref[...], hnt_ref[...], preferred_element_type=F32).astype(BF16)

    def score_body(hc, carry):
        row0 = pl.multiple_of(hc * 128, 128)
        s_ref[hc] = jnp.dot(sk_ref[hc], qt_ref[pl.ds(row0, 128), :], preferred_element_type=F32) * LOG2E
        return carry
    lax.fori_loop(0, 2 * PEER_HEADS, score_body, 0)

    n_grp = n_tc // ROUTE_ILP

    def chunk_cols(it):
        return [pl.ds(pl.multiple_of(((it % n_grp) * ROUTE_ILP + u) * 128, 128), 128) for u in range(ROUTE_ILP)]

    def top_body(it, carry):
        hc = it // n_grp
        cols = chunk_cols(it)
        cur = [s_ref[hc, :, c] for c in cols]
        for k in range(PEER_TOPK):
            m = [jnp.max(x, axis=0, keepdims=True) for x in cur]
            for c, mm in zip(cols, m):
                top_ref[hc, k:k + 1, c] = mm
            cur = [jnp.where(x == mm, NEG_INF, x) for x, mm in zip(cur, m)]
        return carry
    lax.fori_loop(0, 2 * PEER_HEADS * n_grp, top_body, 0)

    row8 = lax.broadcasted_iota(jnp.int32, (8, 128), 0)
    row16 = lax.broadcasted_iota(jnp.int32, (PEER_TOPK, 128), 0)
    n_lead = 4

    def pair_sums(a_rows, a_all, b_all, b_rows):
        out = [a_rows[0] + b_all]
        for i in range(1, n_lead):
            out.append(jnp.where(row8 < PEER_TOPK // (i + 1), a_rows[i] + b_all[0:8], NEG_INF))
        out.append(jnp.where(row16 >= n_lead, a_all + b_rows[0], NEG_INF))
        for j in range(1, PEER_TOPK // (n_lead + 1)):
            out.append(jnp.where(jnp.logical_and(row8 >= n_lead, row8 < PEER_TOPK // (j + 1)),
                                 a_all[0:8] + b_rows[j], NEG_INF))
        return out

    def fold8(x, op):
        return x if x.shape[0] == 8 else op(x[0:8], x[8:16])

    def reduce_all(xs, op, red):
        acc = fold8(xs[0], op)
        for x in xs[1:]:
            acc = op(acc, fold8(x, op))
        return red(acc, axis=0, keepdims=True)

    def cand_body(it, carry):
        hh = it // n_grp
        cols = chunk_cols(it)
        n_b = PEER_TOPK // (n_lead + 1)
        b_all = [top_ref[2 * hh + 1, :, c] for c in cols]
        b_rows = [[top_ref[2 * hh + 1, j:j + 1, c] for j in range(n_b)] for c in cols]
        m = [top_ref[2 * hh, 0:1, c] + br[0] for c, br in zip(cols, b_rows)]
        a_all = [top_ref[2 * hh, :, c] - mm for c, mm in zip(cols, m)]
        a_rows = [[top_ref[2 * hh, i:i + 1, c] - mm for i in range(n_lead)] for c, mm in zip(cols, m)]
        cands = [pair_sums(ar, aa, ba, br) for ar, aa, ba, br in zip(a_rows, a_all, b_all, b_rows)]
        cur = [list(cs) for cs in cands]
        thr = None
        for k in range(PEER_TOPK):
            mx = [reduce_all(cs, jnp.maximum, jnp.max) for cs in cur]
            if k == PEER_TOPK - 1:
                thr = mx
            else:
                cur = [[jnp.where(x == mm, NEG_INF, x) for x in cs] for cs, mm in zip(cur, mx)]
        for u, c in enumerate(cols):
            z = reduce_all([jnp.where(x >= thr[u], jnp.exp2(x), 0.0) for x in cands[u]], jnp.add, jnp.sum)
            logz = jnp.log2(z)
            shifted = pair_sums([r - logz for r in a_rows[u]], a_all[u] - logz, b_all[u], b_rows[u])
            tz = reduce_all([jnp.where(x >= thr[u], y, POS_INF) for x, y in zip(cands[u], shifted)],
                            jnp.minimum, jnp.min)
            thr_ref[hh, :, c] = jnp.broadcast_to(tz, (8, 128))
            s_ref[2 * hh, :, c] = (s_ref[2 * hh, :, c] - m[u]) - logz
        return carry
    lax.fori_loop(0, PEER_HEADS * n_grp, cand_body, 0)


def _peer_kernel(h_ref, g_ref, wqt_ref, sk_ref, u_ref, vt_ref, fg_ref, *refs, n_e, group_tiles):
    n_out = len(group_tiles) if group_tiles else 1
    out_refs = refs[:n_out]
    hnt_ref, qt_ref, s_ref, top_ref, thr_ref, s1b_ref, act0_ref, act1_ref, d0_ref, d1_ref, yt_ref = refs[n_out:]
    _peer_body(h_ref, g_ref, wqt_ref, sk_ref, u_ref, vt_ref, fg_ref, out_refs,
               hnt_ref, qt_ref, s_ref, top_ref, thr_ref, s1b_ref, act0_ref, act1_ref, d0_ref, d1_ref, yt_ref,
               n_e=n_e, group_tiles=group_tiles)


def _peer_body(h_ref, g_ref, wqt_ref, sk_ref, u_ref, vt_ref, fg_ref, out_refs,
               hnt_ref, qt_ref, s_ref, top_ref, thr_ref, s1b_ref, act0_ref, act1_ref, d0_ref, d1_ref, yt_ref,
               *, n_e, group_tiles):
    e = pl.program_id(1)
    tb = h_ref.shape[0]
    n_tc = tb // 128
    nt = u_ref.shape[0]
    n_r = nt // 128

    @pl.when(e == 0)
    def _prologue():
        _peer_route(h_ref, g_ref, wqt_ref, sk_ref, hnt_ref, qt_ref, s_ref, top_ref, thr_ref)
        act0_ref[...] = jnp.dot(u_ref[...], hnt_ref[...], preferred_element_type=F32)
        yt_ref[...] = jnp.zeros_like(yt_ref)

    def body(act_in, act_out, d_in, d_out, do_a=True, do_d=True, do_y=True):
        d_model = vt_ref.shape[0]
        n_q = d_model // Y_ROWS
        k_q = d_model // n_q

        i1_0 = (e - 1) * n_r
        grp0 = pl.multiple_of((i1_0 // 8) * 8, 8)
        sub_iota = lax.broadcasted_iota(jnp.int32, (8, tb), 0)
        for hh in range(PEER_HEADS if do_d else 0):
            grp = s_ref[2 * hh, pl.ds(grp0, 8), :]
            for r in range(n_r):
                row = jnp.sum(jnp.where(sub_iota == i1_0 % 8 + r, grp, 0.0), axis=0, keepdims=True)
                s1b_ref[hh, r] = jnp.broadcast_to(row, (8, tb))

        def d_chunk(r, tc, sub):
            keys = slice(sub * D_SUB, (sub + 1) * D_SUB)
            rows = slice(r * 128 + sub * D_SUB, r * 128 + (sub + 1) * D_SUB)
            cols = slice(tc * 128, (tc + 1) * 128)
            w = jnp.zeros((D_SUB // 8, 8, 128), F32)
            for hh in range(PEER_HEADS):
                s2 = s_ref[2 * hh + 1, keys, cols].reshape(D_SUB // 8, 8, 128)
                x = s2 + s1b_ref[hh, r, :, cols][None]
                w = w + jnp.where(x >= thr_ref[hh, :, cols][None], jnp.exp2(x), 0.0)
            a = act_in[rows, cols]
            gl = 0.5 * a * (1.0 + lax.erf(a * (2.0 ** -0.5)))
            d_out[rows, cols] = (gl * w.reshape(D_SUB, 128)).astype(BF16)

        chunks = [(r, tc, sub) for r in range(n_r) for tc in range(n_tc) for sub in range(128 // D_SUB)]
        per_tick = len(chunks) // (n_q * (tb // MXU_N))
        ci = 0
        for c in range(tb // MXU_N):
            cb = slice(c * MXU_N, (c + 1) * MXU_N)
            acc = None
            for kq in range(n_q):
                kb = slice(kq * k_q, (kq + 1) * k_q)
                if do_a:
                    part = jnp.dot(u_ref[:, kb], hnt_ref[kb, cb], preferred_element_type=F32)
                    acc = part if acc is None else acc + part
                for _ in range(per_tick // 2 if do_d else 0):
                    d_chunk(*chunks[ci]); ci += 1
                rb = slice(kq * Y_ROWS, (kq + 1) * Y_ROWS)
                if do_y:
                    yt_ref[rb, cb] += jnp.dot(vt_ref[rb, :], d_in[:, cb], preferred_element_type=F32)
                for _ in range(per_tick - per_tick // 2 if do_d else 0):
                    d_chunk(*chunks[ci]); ci += 1
            if do_a:
                act_out[:, cb] = acc
        assert ci == (len(chunks) if do_d else 0)

    @pl.when(e == 1)
    def _fill():
        body(act0_ref, act1_ref, d1_ref, d0_ref, do_y=False)

    @pl.when(jnp.logical_and(jnp.logical_and(e > 1, e < n_e), e % 2 == 1))
    def _odd():
        body(act0_ref, act1_ref, d1_ref, d0_ref)

    @pl.when(jnp.logical_and(jnp.logical_and(e > 1, e < n_e), e % 2 == 0))
    def _even():
        body(act1_ref, act0_ref, d0_ref, d1_ref)

    @pl.when(e == n_e)
    def _drain_d():
        body(act1_ref, act0_ref, d0_ref, d1_ref, do_a=False)

    @pl.when(e == n_e + 1)
    def _drain_y():
        body(act0_ref, act1_ref, d1_ref, d0_ref, do_a=False, do_d=False)

    @pl.when(e == n_e + 1)
    def _finish():
        y = h_ref[...] + yt_ref[...].T
        if not group_tiles:
            out_refs[0][...] = y
            return
        y = _rms(y, fg_ref[...])
        t = pl.program_id(0)
        lo = 0
        for o_ref, n in zip(out_refs, group_tiles):
            @pl.when(jnp.logical_and(t >= lo, t < lo + n))
            def _store(o_ref=o_ref):
                o_ref[...] = y
            lo += n


def _peer(h, g, wqt_bf, sk_bf, u_bf, vt_bf, fg, l, group_tiles=None):
    tp, d = h.shape
    ne = u_bf.shape[1]
    tb, nt = PEER_TB, PEER_NT
    n_e = ne // nt
    assert n_e % 2 == 0
    if group_tiles:
        n_tiles = sum(group_tiles)
        starts = [sum(group_tiles[:k]) for k in range(len(group_tiles))]
        out_specs = [pl.BlockSpec((tb, d), lambda t, e, lo=lo, n=n: (jnp.clip(t - lo, 0, n - 1), 0),
                                  pipeline_mode=pl.Buffered(1))
                     for lo, n in zip(starts, group_tiles)]
        out_shape = [jax.ShapeDtypeStruct((n * tb, d), F32) for n in group_tiles]
    else:
        n_tiles = tp // tb
        out_specs = pl.BlockSpec((tb, d), lambda t, e: (t, 0))
        out_shape = jax.ShapeDtypeStruct((tp, d), F32)
    return pl.pallas_call(
        functools.partial(_peer_kernel, n_e=n_e, group_tiles=group_tiles),
        grid=(n_tiles, n_e + 2),
        in_specs=[pl.BlockSpec((tb, d), lambda t, e: (t, 0), pipeline_mode=pl.Buffered(1) if group_tiles else None),
                  pl.BlockSpec((1, d), lambda t, e: (0, 0)),
                  pl.BlockSpec((None,) + wqt_bf.shape[1:], lambda t, e: (l, 0, 0), pipeline_mode=pl.Buffered(1)),
                  pl.BlockSpec((None,) + sk_bf.shape[1:], lambda t, e: (l, 0, 0, 0)),
                  pl.BlockSpec((None, nt, d), lambda t, e: (l, jnp.minimum(e, n_e - 1), 0)),
                  pl.BlockSpec((None, d, nt), lambda t, e: (l, 0, jnp.clip(e - 2, 0, n_e - 1))),
                  pl.BlockSpec((1, d), lambda t, e: (0, 0))],
        out_specs=out_specs,
        out_shape=out_shape,
        scratch_shapes=[pltpu.VMEM((d, tb), BF16),
                        pltpu.VMEM((wqt_bf.shape[1], tb), BF16),
                        pltpu.VMEM((2 * PEER_HEADS, N_KEYS, tb), F32),
                        pltpu.VMEM((2 * PEER_HEADS, PEER_TOPK, tb), F32),
                        pltpu.VMEM((PEER_HEADS, 8, tb), F32),
                        pltpu.VMEM((PEER_HEADS, nt // 128, 8, tb), F32),
                        pltpu.VMEM((nt, tb), F32),
                        pltpu.VMEM((nt, tb), F32),
                        pltpu.VMEM((nt, tb), BF16),
                        pltpu.VMEM((nt, tb), BF16),
                        pltpu.VMEM((d, tb), F32)],
        compiler_params=pltpu.CompilerParams(dimension_semantics=("arbitrary", "arbitrary"),
                                             vmem_limit_bytes=VMEM_LIMIT),
    )(h, g, wqt_bf, sk_bf, u_bf, vt_bf, fg)


def _schedule(seq_lens):
    n_real = [s // BLK for s in seq_lens]
    starts = [sum(n_real[:k]) for k in range(len(n_real))]
    meta0 = sum(n_real)
    n_used = meta0 + len(seq_lens)
    per = max(PEER_TB, IN_TM) // BLK
    nblk = -(-(n_used + 1) // per) * per
    zero = nblk - 1
    jpos, nbs, prv, nxt, mta, fvr = [], [], [], [], [], []
    for k, n in enumerate(n_real):
        for r in range(n):
            i = starts[k] + r
            jpos.append(r + 1); nbs.append(n + 1); mta.append(meta0 + k); fvr.append(0)
            prv.append(meta0 + k if r == 0 else i - 1)
            nxt.append(zero if r == n - 1 else i + 1)
    for k, n in enumerate(n_real):
        jpos.append(0); nbs.append(n + 1); mta.append(meta0 + k); fvr.append(META_ROW0)
        prv.append(zero); nxt.append(starts[k])
    for i in range(n_used, nblk):
        jpos.append(0); nbs.append(1); mta.append(i); fvr.append(BLK)
        prv.append(zero); nxt.append(zero)
    tables = tuple(jnp.asarray(np.asarray(a, np.int32)) for a in (jpos, nbs, prv, nxt, mta, fvr))
    return tables, nblk


def kernel(x_prompt, x_sample, meta_tokens, ln1_g, w_in, conv_a_w, attn_sink, conf_dw_w, conf_dw_b, conf_ln_g,
           conf_ln_b, g_out_a, g_out_b, g_out_c, w_out, ln2_g, peer_wq, peer_subkeys, peer_u, peer_v, final_g):
    d = x_prompt.shape[-1]
    depth = w_in.shape[0]
    groups = (x_prompt, x_sample)
    seq_lens = [x.shape[1] for x in groups for _ in range(x.shape[0])]
    group_rows = [x.shape[0] * x.shape[1] for x in groups]
    assert all(s % BLK == 0 for s in seq_lens) and all(r % PEER_TB == 0 for r in group_rows)
    sched, nblk = _schedule(seq_lens)
    lead = jnp.concatenate([jnp.zeros((META_ROW0, d), F32), meta_tokens.astype(F32)], axis=0)
    n_tail = nblk * BLK - sum(group_rows) - BLK * len(seq_lens)
    h = jnp.concatenate([x.reshape(-1, d) for x in groups] + [lead] * len(seq_lens) + [jnp.zeros((n_tail, d), F32)],
                        axis=0)

    w_in_bf = w_in.astype(BF16)
    w_out_bf = w_out.astype(BF16)
    wqt_bf = jnp.swapaxes(peer_wq, 1, 2).astype(BF16)
    sk_bf = peer_subkeys.reshape(depth, 2 * PEER_HEADS, N_KEYS, -1).astype(BF16)
    u_bf = peer_u.astype(BF16)
    vt_bf = jnp.swapaxes(peer_v, 1, 2).astype(BF16)

    row = lambda a: a.reshape(1, -1)
    for l in range(depth):
        ua, bg, q, kv, glu = _inproj(h, row(ln1_g[l]), w_in_bf, l)
        h = _mixer(sched, ua, bg, q, kv, glu, h, conv_a_w[l], attn_sink[l], conf_dw_w[l], row(conf_dw_b[l]),
                   row(conf_ln_g[l]), row(conf_ln_b[l]), row(g_out_a[l]), row(g_out_b[l]), row(g_out_c[l]),
                   w_out_bf, l)
        last = l == depth - 1
        h = _peer(h, row(ln2_g[l]), wqt_bf, sk_bf, u_bf, vt_bf, row(final_g), l,
                  group_tiles=tuple(r // PEER_TB for r in group_rows) if last else None)
    return tuple(y.reshape(x.shape) for y, x in zip(h, groups))
```

```python
import functools

import numpy as np
import jax
import jax.numpy as jnp
from jax import lax
from jax.experimental import pallas as pl
from jax.experimental.pallas import tpu as pltpu

F32 = jnp.float32
BF16 = jnp.bfloat16

N_META = 16
D_SCONV = 512
N_Q_HEADS = 16
N_KV_HEADS = 4
HEAD_DIM = 64
GQA_GROUP = N_Q_HEADS // N_KV_HEADS
D_ATTN = N_Q_HEADS * HEAD_DIM
D_KV = N_KV_HEADS * HEAD_DIM
D_CONF = 512
SHORT_CONV_W = 3
CONF_CONV_W = 31
WINDOW = 128
BLK = 128
META_ROW0 = BLK - N_META
N_KEYS = 128
PEER_HEADS = 8
PEER_TOPK = 16
RMS_EPS = 1e-6
LN_EPS = 1e-5
NEG = -1e30
NEG_INF = float("-inf")
POS_INF = float("inf")
LOG2E = 1.4426950408889634

IN_TM = 256
PEER_TB = 512
PEER_NT = 512
MXU_N = 256
Y_ROWS = 256
ATT_ROWS = 32
ATT_ILP = 2
ROUTE_ILP = 2
D_SUB = 32
VMEM_LIMIT = 56 * 1024 * 1024


def _rms(x, g):
    return x * lax.rsqrt(jnp.mean(x * x, axis=-1, keepdims=True) + RMS_EPS) * g


def _inproj_kernel(h_ref, g_ref, w_ref, ua_ref, bg_ref, q_ref, kv_ref, glu_ref):
    xn = _rms(h_ref[...], g_ref[...]).astype(BF16)
    z = jnp.dot(xn, w_ref[...], preferred_element_type=F32)
    o = 0
    a_x = z[:, o:o + D_SCONV]; o += D_SCONV
    a_bg = z[:, o:o + D_SCONV]; o += D_SCONV
    a_cg = z[:, o:o + D_SCONV]; o += D_SCONV
    q = z[:, o:o + D_ATTN]; o += D_ATTN
    kv = z[:, o:o + 2 * D_KV]; o += 2 * D_KV
    c_a = z[:, o:o + D_CONF]; o += D_CONF
    c_g = z[:, o:o + D_CONF]
    ua_ref[...] = a_cg * a_x
    bg_ref[...] = a_bg
    q_ref[...] = (q * (HEAD_DIM ** -0.5 * LOG2E)).astype(BF16)
    kv_ref[...] = kv.astype(BF16)
    glu_ref[...] = c_a * jax.nn.sigmoid(c_g)


def _inproj(h, g, w_bf, l):
    tp, d = h.shape
    n = w_bf.shape[2]
    row = lambda w: pl.BlockSpec((IN_TM, w), lambda i: (i, 0))
    return pl.pallas_call(
        _inproj_kernel,
        grid=(tp // IN_TM,),
        in_specs=[row(d),
                  pl.BlockSpec((1, d), lambda i: (0, 0)),
                  pl.BlockSpec((None, d, n), lambda i: (l, 0, 0), pipeline_mode=pl.Buffered(1))],
        out_specs=[row(D_SCONV), row(D_SCONV), row(D_ATTN), row(2 * D_KV), row(D_CONF)],
        out_shape=[jax.ShapeDtypeStruct((tp, D_SCONV), F32),
                   jax.ShapeDtypeStruct((tp, D_SCONV), F32),
                   jax.ShapeDtypeStruct((tp, D_ATTN), BF16),
                   jax.ShapeDtypeStruct((tp, 2 * D_KV), BF16),
                   jax.ShapeDtypeStruct((tp, D_CONF), F32)],
        compiler_params=pltpu.CompilerParams(dimension_semantics=("arbitrary",),
                                             vmem_limit_bytes=VMEM_LIMIT),
    )(h, g, w_bf)


def _mixer_kernel(jpos_ref, nbs_ref, prv_ref, nxt_ref, mta_ref, fvr_ref,
                  ua_p, ua_c, ua_n, bg_ref, q_ref, kv_p, kv_c, kv_n, kv_m, glu_p, glu_c, glu_n, h_ref,
                  cw_ref, sink_ref, dw_ref, db_ref, lg_ref, lb_ref, ga_ref, gb_ref, gc_ref, wo_ref,
                  out_ref, xa_ref, xc_ref, cc_ref, yb_ref, dist_ref, s_ref, p_ref):
    i = pl.program_id(0)
    j = jpos_ref[i]
    nb = nbs_ref[i]

    xa_ref[0:8, :] = ua_p[...]
    xa_ref[8:8 + BLK, :] = ua_c[...]
    xa_ref[8 + BLK:16 + BLK, :] = ua_n[...]
    conv_a = jnp.zeros((BLK, D_SCONV), F32)
    for k in range(SHORT_CONV_W):
        conv_a = conv_a + cw_ref[k:k + 1, :] * xa_ref[pl.ds(8 + k - SHORT_CONV_W // 2, BLK), :]
    y_a = _rms(bg_ref[...] * conv_a, ga_ref[...])

    n_grp = BLK // 8
    n_slab = D_CONF // 128
    for cs in range(n_slab):
        cols = slice(cs * 128, (cs + 1) * 128)
        xc_ref[cs, 0:16, :] = glu_p[:, cols]
        xc_ref[cs, 16:16 + BLK, :] = glu_c[:, cols]
        xc_ref[cs, 16 + BLK:32 + BLK, :] = glu_n[:, cols]
        accs = [jnp.zeros((8, 128), F32) for _ in range(n_grp)]
        for k in range(CONF_CONV_W):
            wk = jnp.broadcast_to(dw_ref[k:k + 1, cols], (8, 128))
            for g in range(n_grp):
                accs[g] = accs[g] + wk * xc_ref[cs, pl.ds(g + k + 16 - CONF_CONV_W // 2, 8, stride=n_grp), :]
        for g in range(n_grp):
            cc_ref[cs, g * 8:(g + 1) * 8, :] = accs[g] + db_ref[:, cols]
    conv_c = jnp.concatenate([cc_ref[cs] for cs in range(n_slab)], axis=1)
    mu = jnp.mean(conv_c, axis=-1, keepdims=True)
    xc = conv_c - mu
    var = jnp.mean(xc * xc, axis=-1, keepdims=True)
    ln = xc * lax.rsqrt(var + LN_EPS) * lg_ref[...] + lb_ref[...]
    y_c_strided = _rms(ln * jax.nn.sigmoid(ln), gc_ref[...])
    for cs in range(n_slab):
        cc_ref[cs] = y_c_strided[:, cs * 128:(cs + 1) * 128]
    y_c = jnp.concatenate(
        [jnp.concatenate([cc_ref[cs, pl.ds((q % 2) * (BLK // 2) + q // 2, 8, stride=8), :]
                          for cs in range(n_slab)], axis=1) for q in range(n_grp)], axis=0)

    n_keys = 4 * BLK
    rq = lax.broadcasted_iota(jnp.int32, (BLK, n_keys), 0)
    ck = lax.broadcasted_iota(jnp.int32, (BLK, n_keys), 1)
    rk = ck % BLK
    kb = ck // BLK
    jb = jnp.full((BLK, n_keys), j, jnp.int32)
    qpos = jb * BLK - META_ROW0 + rq
    dist = jnp.where(kb == 0, rq + BLK - rk,
                     jnp.where(kb == 1, jnp.abs(rq - rk),
                               jnp.where(kb == 2, BLK + rk - rq,
                                         jnp.minimum(jnp.abs(qpos - (rk - META_ROW0)), WINDOW))))
    both = jnp.logical_and
    ok = jnp.logical_or(
        jnp.logical_or(both(kb == 0, both(rk >= rq, jb >= 2)), both(kb == 1, jb >= 1)),
        jnp.logical_or(both(kb == 2, both(rk <= rq, jb <= nb - 2)), both(kb == 3, rk >= META_ROW0)))
    dist_ref[...] = jnp.where(ok, dist, -1).astype(F32)
    sink_col = lax.broadcasted_iota(jnp.int32, (ATT_ROWS, n_keys), 1) == 3 * BLK
    nt = (((1,), (1,)), ((), ()))
    lane = lax.broadcasted_iota(jnp.int32, (1, 2 * HEAD_DIM), 1)

    def value_tile(ref, g):
        slab = ref[:, D_KV + (g // 2) * 2 * HEAD_DIM:D_KV + (g // 2 + 1) * 2 * HEAD_DIM]
        own = (lane < HEAD_DIM) if g % 2 == 0 else (lane >= HEAD_DIM)
        ones_at = HEAD_DIM if g % 2 == 0 else 0
        return jnp.where(own, slab, (lane == ones_at).astype(BF16))

    for g in range(N_KV_HEADS):
        ks = slice(g * HEAD_DIM, (g + 1) * HEAD_DIM)
        heads = range(g * GQA_GROUP, (g + 1) * GQA_GROUP)
        q_grp = jnp.concatenate([q_ref[:, hd * HEAD_DIM:(hd + 1) * HEAD_DIM] for hd in heads], axis=0)
        k_all = jnp.concatenate([kv_p[:, ks], kv_c[:, ks], kv_n[:, ks], kv_m[:, ks]], axis=0)
        v_all = jnp.concatenate([value_tile(kv_p, g), value_tile(kv_c, g), value_tile(kv_n, g),
                                 value_tile(kv_m, g)], axis=0)
        s_ref[...] = lax.dot_general(q_grp, k_all, nt, preferred_element_type=F32)
        for hq, hd in enumerate(heads):
            slope = float(2.0 ** (-8.0 * (hd + 1) / N_Q_HEADS)) * LOG2E
            sink = sink_ref[hd] * LOG2E
            for r0 in range(0, BLK, ATT_ROWS * ATT_ILP):
                local = [slice(r0 + u * ATT_ROWS, r0 + (u + 1) * ATT_ROWS) for u in range(ATT_ILP)]
                rows = [slice(hq * BLK + lc.start, hq * BLK + lc.stop) for lc in local]
                dd = [dist_ref[lc, :] for lc in local]
                sc = [jnp.where(d >= 0.0, s_ref[rw, :] - slope * d, NEG) for rw, d in zip(rows, dd)]
                sc = [jnp.where(sink_col, sink, x) for x in sc]
                mx = [jnp.max(jnp.maximum(jnp.maximum(x[:, 0:BLK], x[:, BLK:2 * BLK]),
                                          jnp.maximum(x[:, 2 * BLK:3 * BLK], x[:, 3 * BLK:])),
                              axis=-1, keepdims=True) for x in sc]
                for rw, x, m in zip(rows, sc, mx):
                    p_ref[rw, :] = jnp.exp2(x - m).astype(BF16)
        o_all = jnp.dot(p_ref[...], v_all, preferred_element_type=F32)
        o_lo = 0 if g % 2 == 0 else HEAD_DIM
        sum_at = HEAD_DIM if g % 2 == 0 else 0
        for hq, hd in enumerate(heads):
            o = o_all[hq * BLK:(hq + 1) * BLK]
            yb_ref[:, hd * HEAD_DIM:(hd + 1) * HEAD_DIM] = o[:, o_lo:o_lo + HEAD_DIM] / o[:, sum_at:sum_at + 1]
    y_b = _rms(yb_ref[...], gb_ref[...])

    out = (jnp.dot(y_a.astype(BF16), wo_ref[0:D_SCONV, :], preferred_element_type=F32)
           + jnp.dot(y_b.astype(BF16), wo_ref[D_SCONV:D_SCONV + D_ATTN, :], preferred_element_type=F32)
           + jnp.dot(y_c.astype(BF16), wo_ref[D_SCONV + D_ATTN:, :], preferred_element_type=F32))
    keep = lax.broadcasted_iota(jnp.int32, (BLK, 1), 0) >= fvr_ref[i]
    out_ref[...] = jnp.where(keep, h_ref[...] + out, 0.0)


def _mixer(sched, ua, bg, q, kv, glu, h, cw, sink, dw, db, lg, lb, ga, gb, gc, wo_bf, l):
    tp, d = h.shape
    nblk = tp // BLK
    cur = lambda w: pl.BlockSpec((BLK, w), lambda i, *_: (i, 0))
    prv = lambda w: pl.BlockSpec((BLK, w), lambda i, jp, nb, pv, nx, mt, fv: (pv[i], 0))
    nxt = lambda w: pl.BlockSpec((BLK, w), lambda i, jp, nb, pv, nx, mt, fv: (nx[i], 0))
    halo_p = lambda r, w: pl.BlockSpec((r, w), lambda i, jp, nb, pv, nx, mt, fv: ((pv[i] + 1) * (BLK // r) - 1, 0))
    halo_n = lambda r, w: pl.BlockSpec((r, w), lambda i, jp, nb, pv, nx, mt, fv: (nx[i] * (BLK // r), 0))
    meta = lambda w: pl.BlockSpec((BLK, w), lambda i, jp, nb, pv, nx, mt, fv: (mt[i], 0))
    full = lambda a: pl.BlockSpec(a.shape, lambda i, *_: (0,) * a.ndim)
    grid_spec = pltpu.PrefetchScalarGridSpec(
        num_scalar_prefetch=len(sched),
        grid=(nblk,),
        in_specs=[halo_p(8, D_SCONV), cur(D_SCONV), halo_n(8, D_SCONV), cur(D_SCONV), cur(D_ATTN),
                  prv(2 * D_KV), cur(2 * D_KV), nxt(2 * D_KV), meta(2 * D_KV),
                  halo_p(16, D_CONF), cur(D_CONF), halo_n(16, D_CONF), cur(d),
                  full(cw), pl.BlockSpec(memory_space=pltpu.SMEM), full(dw), full(db), full(lg), full(lb),
                  full(ga), full(gb), full(gc),
                  pl.BlockSpec((None,) + wo_bf.shape[1:], lambda i, *_: (l, 0, 0), pipeline_mode=pl.Buffered(1))],
        out_specs=cur(d),
        scratch_shapes=[pltpu.VMEM((BLK + 16, D_SCONV), F32),
                        pltpu.VMEM((D_CONF // 128, BLK + 32, 128), F32),
                        pltpu.VMEM((D_CONF // 128, BLK, 128), F32),
                        pltpu.VMEM((BLK, D_ATTN), F32),
                        pltpu.VMEM((BLK, 4 * BLK), F32),
                        pltpu.VMEM((GQA_GROUP * BLK, 4 * BLK), F32),
                        pltpu.VMEM((GQA_GROUP * BLK, 4 * BLK), BF16)],
    )
    return pl.pallas_call(
        _mixer_kernel,
        grid_spec=grid_spec,
        out_shape=jax.ShapeDtypeStruct((tp, d), F32),
        compiler_params=pltpu.CompilerParams(dimension_semantics=("arbitrary",),
                                             vmem_limit_bytes=VMEM_LIMIT),
    )(*sched, ua, ua, ua, bg, q, kv, kv, kv, kv, glu, glu, glu, h,
      cw, sink, dw, db, lg, lb, ga, gb, gc, wo_bf)


def _peer_route(h_ref, g_ref, wqt_ref, sk_ref, hnt_ref, qt_ref, s_ref, top_ref, thr_ref):
    tb = h_ref.shape[0]
    n_tc = tb // 128
    xn = _rms(h_ref[...], g_ref[...])
    hnt_ref[...] = xn.T.astype(BF16)
    qt_ref[...] = jnp.dot(wqt_ref[...], hnt_ref[...], preferred_element_type=F32).astype(BF16)

    def score_body(hc, carry):
        row0 = pl.multiple_of(hc * 128, 128)
        s_ref[hc] = jnp.dot(sk_ref[hc], qt_ref[pl.ds(row0, 128), :], preferred_element_type=F32) * LOG2E
        return carry
    lax.fori_loop(0, 2 * PEER_HEADS, score_body, 0)

    n_grp = n_tc // ROUTE_ILP

    def chunk_cols(it):
        return [pl.ds(pl.multiple_of(((it % n_grp) * ROUTE_ILP + u) * 128, 128), 128) for u in range(ROUTE_ILP)]

    def top_body(it, carry):
        hc = it // n_grp
        cols = chunk_cols(it)
        cur = [s_ref[hc, :, c] for c in cols]
        for k in range(PEER_TOPK):
            m = [jnp.max(x, axis=0, keepdims=True) for x in cur]
            for c, mm in zip(cols, m):
                top_ref[hc, k:k + 1, c] = mm
            cur = [jnp.where(x == mm, NEG_INF, x) for x, mm in zip(cur, m)]
        return carry
    lax.fori_loop(0, 2 * PEER_HEADS * n_grp, top_body, 0)

    row8 = lax.broadcasted_iota(jnp.int32, (8, 128), 0)
    row16 = lax.broadcasted_iota(jnp.int32, (PEER_TOPK, 128), 0)
    n_lead = 4

    def pair_sums(a_rows, a_all, b_all, b_rows):
        out = [a_rows[0] + b_all]
        for i in range(1, n_lead):
            out.append(jnp.where(row8 < PEER_TOPK // (i + 1), a_rows[i] + b_all[0:8], NEG_INF))
        out.append(jnp.where(row16 >= n_lead, a_all + b_rows[0], NEG_INF))
        for j in range(1, PEER_TOPK // (n_lead + 1)):
            out.append(jnp.where(jnp.logical_and(row8 >= n_lead, row8 < PEER_TOPK // (j + 1)),
                                 a_all[0:8] + b_rows[j], NEG_INF))
        return out

    def fold8(x, op):
        return x if x.shape[0] == 8 else op(x[0:8], x[8:16])

    def reduce_all(xs, op, red):
        acc = fold8(xs[0], op)
        for x in xs[1:]:
            acc = op(acc, fold8(x, op))
        return red(acc, axis=0, keepdims=True)

    def cand_body(it, carry):
        hh = it // n_grp
        cols = chunk_cols(it)
        n_b = PEER_TOPK // (n_lead + 1)
        b_all = [top_ref[2 * hh + 1, :, c] for c in cols]
        b_rows = [[top_ref[2 * hh + 1, j:j + 1, c] for j in range(n_b)] for c in cols]
        m = [top_ref[2 * hh, 0:1, c] + br[0] for c, br in zip(cols, b_rows)]
        a_all = [top_ref[2 * hh, :, c] - mm for c, mm in zip(cols, m)]
        a_rows = [[top_ref[2 * hh, i:i + 1, c] - mm for i in range(n_lead)] for c, mm in zip(cols, m)]
        cands = [pair_sums(ar, aa, ba, br) for ar, aa, ba, br in zip(a_rows, a_all, b_all, b_rows)]
        cur = [list(cs) for cs in cands]
        thr = None
        for k in range(PEER_TOPK):
            mx = [reduce_all(cs, jnp.maximum, jnp.max) for cs in cur]
            if k == PEER_TOPK - 1:
                thr = mx
            else:
                cur = [[jnp.where(x == mm, NEG_INF, x) for x in cs] for cs, mm in zip(cur, mx)]
        for u, c in enumerate(cols):
            z = reduce_all([jnp.where(x >= thr[u], jnp.exp2(x), 0.0) for x in cands[u]], jnp.add, jnp.sum)
            logz = jnp.log2(z)
            shifted = pair_sums([r - logz for r in a_rows[u]], a_all[u] - logz, b_all[u], b_rows[u])
            tz = reduce_all([jnp.where(x >= thr[u], y, POS_INF) for x, y in zip(cands[u], shifted)],
                            jnp.minimum, jnp.min)
            thr_ref[hh, :, c] = jnp.broadcast_to(tz, (8, 128))
            s_ref[2 * hh, :, c] = (s_ref[2 * hh, :, c] - m[u]) - logz
        return carry
    lax.fori_loop(0, PEER_HEADS * n_grp, cand_body, 0)


def _peer_kernel(h_ref, g_ref, wqt_ref, sk_ref, u_ref, vt_ref, fg_ref, *refs, n_e, group_tiles):
    n_out = len(group_tiles) if group_tiles else 1
    out_refs = refs[:n_out]
    hnt_ref, qt_ref, s_ref, top_ref, thr_ref, s1b_ref, act0_ref, act1_ref, d0_ref, d1_ref, yt_ref = refs[n_out:]
    _peer_body(h_ref, g_ref, wqt_ref, sk_ref, u_ref, vt_ref, fg_ref, out_refs,
               hnt_ref, qt_ref, s_ref, top_ref, thr_ref, s1b_ref, act0_ref, act1_ref, d0_ref, d1_ref, yt_ref,
               n_e=n_e, group_tiles=group_tiles)


def _peer_body(h_ref, g_ref, wqt_ref, sk_ref, u_ref, vt_ref, fg_ref, out_refs,
               hnt_ref, qt_ref, s_ref, top_ref, thr_ref, s1b_ref, act0_ref, act1_ref, d0_ref, d1_ref, yt_ref,
               *, n_e, group_tiles):
    e = pl.program_id(1)
    tb = h_ref.shape[0]
    n_tc = tb // 128
    nt = u_ref.shape[0]
    n_r = nt // 128

    @pl.when(e == 0)
    def _prologue():
        _peer_route(h_ref, g_ref, wqt_ref, sk_ref, hnt_ref, qt_ref, s_ref, top_ref, thr_ref)
        act0_ref[...] = jnp.dot(u_ref[...], hnt_ref[...], preferred_element_type=F32)
        yt_ref[...] = jnp.zeros_like(yt_ref)

    def body(act_in, act_out, d_in, d_out, do_a=True, do_d=True, do_y=True):
        d_model = vt_ref.shape[0]
        n_q = d_model // Y_ROWS
        k_q = d_model // n_q

        i1_0 = (e - 1) * n_r
        grp0 = pl.multiple_of((i1_0 // 8) * 8, 8)
        sub_iota = lax.broadcasted_iota(jnp.int32, (8, tb), 0)
        for hh in range(PEER_HEADS if do_d else 0):
            grp = s_ref[2 * hh, pl.ds(grp0, 8), :]
            for r in range(n_r):
                row = jnp.sum(jnp.where(sub_iota == i1_0 % 8 + r, grp, 0.0), axis=0, keepdims=True)
                s1b_ref[hh, r] = jnp.broadcast_to(row, (8, tb))

        def d_chunk(r, tc, sub):
            keys = slice(sub * D_SUB, (sub + 1) * D_SUB)
            rows = slice(r * 128 + sub * D_SUB, r * 128 + (sub + 1) * D_SUB)
            cols = slice(tc * 128, (tc + 1) * 128)
            w = jnp.zeros((D_SUB // 8, 8, 128), F32)
            for hh in range(PEER_HEADS):
                s2 = s_ref[2 * hh + 1, keys, cols].reshape(D_SUB // 8, 8, 128)
                x = s2 + s1b_ref[hh, r, :, cols][None]
                w = w + jnp.where(x >= thr_ref[hh, :, cols][None], jnp.exp2(x), 0.0)
            a = act_in[rows, cols]
            gl = 0.5 * a * (1.0 + lax.erf(a * (2.0 ** -0.5)))
            d_out[rows, cols] = (gl * w.reshape(D_SUB, 128)).astype(BF16)

        chunks = [(r, tc, sub) for r in range(n_r) for tc in range(n_tc) for sub in range(128 // D_SUB)]
        per_tick = len(chunks) // (n_q * (tb // MXU_N))
        ci = 0
        for c in range(tb // MXU_N):
            cb = slice(c * MXU_N, (c + 1) * MXU_N)
            acc = None
            for kq in range(n_q):
                kb = slice(kq * k_q, (kq + 1) * k_q)
                if do_a:
                    part = jnp.dot(u_ref[:, kb], hnt_ref[kb, cb], preferred_element_type=F32)
                    acc = part if acc is None else acc + part
                for _ in range(per_tick // 2 if do_d else 0):
                    d_chunk(*chunks[ci]); ci += 1
                rb = slice(kq * Y_ROWS, (kq + 1) * Y_ROWS)
                if do_y:
                    yt_ref[rb, cb] += jnp.dot(vt_ref[rb, :], d_in[:, cb], preferred_element_type=F32)
                for _ in range(per_tick - per_tick // 2 if do_d else 0):
                    d_chunk(*chunks[ci]); ci += 1
            if do_a:
                act_out[:, cb] = acc
        assert ci == (len(chunks) if do_d else 0)

    @pl.when(e == 1)
    def _fill():
        body(act0_ref, act1_ref, d1_ref, d0_ref, do_y=False)

    @pl.when(jnp.logical_and(jnp.logical_and(e > 1, e < n_e), e % 2 == 1))
    def _odd():
        body(act0_ref, act1_ref, d1_ref, d0_ref)

    @pl.when(jnp.logical_and(jnp.logical_and(e > 1, e < n_e), e % 2 == 0))
    def _even():
        body(act1_ref, act0_ref, d0_ref, d1_ref)

    @pl.when(e == n_e)
    def _drain_d():
        body(act1_ref, act0_ref, d0_ref, d1_ref, do_a=False)

    @pl.when(e == n_e + 1)
    def _drain_y():
        body(act0_ref, act1_ref, d1_ref, d0_ref, do_a=False, do_d=False)

    @pl.when(e == n_e + 1)
    def _finish():
        y = h_ref[...] + yt_ref[...].T
        if not group_tiles:
            out_refs[0][...] = y
            return
        y = _rms(y, fg_ref[...])
        t = pl.program_id(0)
        lo = 0
        for o_ref, n in zip(out_refs, group_tiles):
            @pl.when(jnp.logical_and(t >= lo, t < lo + n))
            def _store(o_ref=o_ref):
                o_ref[...] = y
            lo += n


def _peer(h, g, wqt_bf, sk_bf, u_bf, vt_bf, fg, l, group_tiles=None):
    tp, d = h.shape
    ne = u_bf.shape[1]
    tb, nt = PEER_TB, PEER_NT
    n_e = ne // nt
    assert n_e % 2 == 0
    if group_tiles:
        n_tiles = sum(group_tiles)
        starts = [sum(group_tiles[:k]) for k in range(len(group_tiles))]
        out_specs = [pl.BlockSpec((tb, d), lambda t, e, lo=lo, n=n: (jnp.clip(t - lo, 0, n - 1), 0),
                                  pipeline_mode=pl.Buffered(1))
                     for lo, n in zip(starts, group_tiles)]
        out_shape = [jax.ShapeDtypeStruct((n * tb, d), F32) for n in group_tiles]
    else:
        n_tiles = tp // tb
        out_specs = pl.BlockSpec((tb, d), lambda t, e: (t, 0))
        out_shape = jax.ShapeDtypeStruct((tp, d), F32)
    return pl.pallas_call(
        functools.partial(_peer_kernel, n_e=n_e, group_tiles=group_tiles),
        grid=(n_tiles, n_e + 2),
        in_specs=[pl.BlockSpec((tb, d), lambda t, e: (t, 0), pipeline_mode=pl.Buffered(1) if group_tiles else None),
                  pl.BlockSpec((1, d), lambda t, e: (0, 0)),
                  pl.BlockSpec((None,) + wqt_bf.shape[1:], lambda t, e: (l, 0, 0), pipeline_mode=pl.Buffered(1)),
                  pl.BlockSpec((None,) + sk_bf.shape[1:], lambda t, e: (l, 0, 0, 0)),
                  pl.BlockSpec((None, nt, d), lambda t, e: (l, jnp.minimum(e, n_e - 1), 0)),
                  pl.BlockSpec((None, d, nt), lambda t, e: (l, 0, jnp.clip(e - 2, 0, n_e - 1))),
                  pl.BlockSpec((1, d), lambda t, e: (0, 0))],
        out_specs=out_specs,
        out_shape=out_shape,
        scratch_shapes=[pltpu.VMEM((d, tb), BF16),
                        pltpu.VMEM((wqt_bf.shape[1], tb), BF16),
                        pltpu.VMEM((2 * PEER_HEADS, N_KEYS, tb), F32),
                        pltpu.VMEM((2 * PEER_HEADS, PEER_TOPK, tb), F32),
                        pltpu.VMEM((PEER_HEADS, 8, tb), F32),
                        pltpu.VMEM((PEER_HEADS, nt // 128, 8, tb), F32),
                        pltpu.VMEM((nt, tb), F32),
                        pltpu.VMEM((nt, tb), F32),
                        pltpu.VMEM((nt, tb), BF16),
                        pltpu.VMEM((nt, tb), BF16),
                        pltpu.VMEM((d, tb), F32)],
        compiler_params=pltpu.CompilerParams(dimension_semantics=("arbitrary", "arbitrary"),
                                             vmem_limit_bytes=VMEM_LIMIT),
    )(h, g, wqt_bf, sk_bf, u_bf, vt_bf, fg)


def _schedule(seq_lens):
    n_real = [s // BLK for s in seq_lens]
    starts = [sum(n_real[:k]) for k in range(len(n_real))]
    meta0 = sum(n_real)
    n_used = meta0 + len(seq_lens)
    per = max(PEER_TB, IN_TM) // BLK
    nblk = -(-(n_used + 1) // per) * per
    zero = nblk - 1
    jpos, nbs, prv, nxt, mta, fvr = [], [], [], [], [], []
    for k, n in enumerate(n_real):
        for r in range(n):
            i = starts[k] + r
            jpos.append(r + 1); nbs.append(n + 1); mta.append(meta0 + k); fvr.append(0)
            prv.append(meta0 + k if r == 0 else i - 1)
            nxt.append(zero if r == n - 1 else i + 1)
    for k, n in enumerate(n_real):
        jpos.append(0); nbs.append(n + 1); mta.append(meta0 + k); fvr.append(META_ROW0)
        prv.append(zero); nxt.append(starts[k])
    for i in range(n_used, nblk):
        jpos.append(0); nbs.append(1); mta.append(i); fvr.append(BLK)
        prv.append(zero); nxt.append(zero)
    tables = tuple(jnp.asarray(np.asarray(a, np.int32)) for a in (jpos, nbs, prv, nxt, mta, fvr))
    return tables, nblk


def kernel(x_prompt, x_sample, meta_tokens, ln1_g, w_in, conv_a_w, attn_sink, conf_dw_w, conf_dw_b, conf_ln_g,
           conf_ln_b, g_out_a, g_out_b, g_out_c, w_out, ln2_g, peer_wq, peer_subkeys, peer_u, peer_v, final_g):
    d = x_prompt.shape[-1]
    depth = w_in.shape[0]
    groups = (x_prompt, x_sample)
    seq_lens = [x.shape[1] for x in groups for _ in range(x.shape[0])]
    group_rows = [x.shape[0] * x.shape[1] for x in groups]
    assert all(s % BLK == 0 for s in seq_lens) and all(r % PEER_TB == 0 for r in group_rows)
    sched, nblk = _schedule(seq_lens)
    lead = jnp.concatenate([jnp.zeros((META_ROW0, d), F32), meta_tokens.astype(F32)], axis=0)
    n_tail = nblk * BLK - sum(group_rows) - BLK * len(seq_lens)
    h = jnp.concatenate([x.reshape(-1, d) for x in groups] + [lead] * len(seq_lens) + [jnp.zeros((n_tail, d), F32)],
                        axis=0)

    w_in_bf = w_in.astype(BF16)
    w_out_bf = w_out.astype(BF16)
    wqt_bf = jnp.swapaxes(peer_wq, 1, 2).astype(BF16)
    sk_bf = peer_subkeys.reshape(depth, 2 * PEER_HEADS, N_KEYS, -1).astype(BF16)
    u_bf = peer_u.astype(BF16)
    vt_bf = jnp.swapaxes(peer_v, 1, 2).astype(BF16)

    row = lambda a: a.reshape(1, -1)
    for l in range(depth):
        ua, bg, q, kv, glu = _inproj(h, row(ln1_g[l]), w_in_bf, l)
        h = _mixer(sched, ua, bg, q, kv, glu, h, conv_a_w[l], attn_sink[l], conf_dw_w[l], row(conf_dw_b[l]),
                   row(conf_ln_g[l]), row(conf_ln_b[l]), row(g_out_a[l]), row(g_out_b[l]), row(g_out_c[l]),
                   w_out_bf, l)
        last = l == depth - 1
        h = _peer(h, row(ln2_g[l]), wqt_bf, sk_bf, u_bf, vt_bf, row(final_g), l,
                  group_tiles=tuple(r // PEER_TB for r in group_rows) if last else None)
    return tuple(y.reshape(x.shape) for y, x in zip(h, groups))
```
